```python
import functools
import jax, jax.numpy as jnp
from jax import lax
import numpy as np

D_MODEL = 1024
BATCH = 2
SEQ = 8192
DEPTH = 1
DEC_BATCH = 128
DEC_SEQ = 8
PAST_LEN = 8192
PAGE_SIZE = 128

N_HEADS = 8
QK_NOPE = 64
QK_ROPE = 32
V_HEAD = 64
Q_RANK = 384
KV_RANK = 256
MLA_WIDTH = N_HEADS * V_HEAD
QK_SCALE = (QK_NOPE + QK_ROPE) ** -0.5
ROPE_BASE = 10000.0
Q_BLOCK = 128
N_GROUPS = 8
GROUP_DIM = 64
GMLP_WIDTH = N_GROUPS * GROUP_DIM
CHUNK = 128
OFF_KV = Q_RANK
OFF_KR = OFF_KV + KV_RANK
OFF_G = OFF_KR + QK_ROPE
IN_COLS = OFF_G + 2 * GMLP_WIDTH
MIX_WIDTH = MLA_WIDTH + GMLP_WIDTH
D_FF = -(-8 * D_MODEL // (3 * 256)) * 256
EPS = 1e-6

kernel_name = "hymba_mla_gmlp_decoder_step"


def rmsnorm(x, g):
    xf = x.astype(jnp.float32)
    y = xf * lax.rsqrt(jnp.mean(xf * xf, axis=-1, keepdims=True) + EPS)
    return (y * g.astype(jnp.float32)).astype(x.dtype)


def layernorm(x, g, b):
    xf = x.astype(jnp.float32)
    mu = jnp.mean(xf, axis=-1, keepdims=True)
    xc = xf - mu
    y = xc * lax.rsqrt(jnp.mean(xc * xc, axis=-1, keepdims=True) + EPS)
    return (y * g.astype(jnp.float32) + b.astype(jnp.float32)).astype(x.dtype)


def rope(x, pos):
    half = QK_ROPE // 2
    inv = ROPE_BASE ** (-jnp.arange(half, dtype=jnp.float32) / half)
    ang = pos.astype(jnp.float32)[:, None] * inv[None, :]
    bshape = (ang.shape[0],) + (1,) * (x.ndim - 3) + (half,)
    c = jnp.cos(ang).reshape(bshape)
    s = jnp.sin(ang).reshape(bshape)
    xf = x.astype(jnp.float32)
    x1, x2 = xf[..., :half], xf[..., half:]
    return jnp.concatenate([x1 * c - x2 * s, x1 * s + x2 * c], axis=-1).astype(x.dtype)


def mixer_front(h, pos, w_in, q_norm_g, w_q_up, kv_norm_g, w_uk, v_norm_g, v_norm_b):
    B, S, _ = h.shape
    z = h @ w_in
    c_q = z[..., :OFF_KV]
    c_kv = rmsnorm(z[..., OFF_KV:OFF_KR], kv_norm_g)
    k_r = rope(z[..., OFF_KR:OFF_G], pos)
    q = (rmsnorm(c_q, q_norm_g) @ w_q_up).reshape(B, S, N_HEADS, QK_NOPE + QK_ROPE)
    q_lat = jnp.einsum('bshn,rhn->bshr', q[..., :QK_NOPE], w_uk) * QK_SCALE
    q_rope = rope(q[..., QK_NOPE:], pos) * QK_SCALE
    zg = jax.nn.gelu(z[..., OFF_G:], approximate=False)
    u = zg[..., :GMLP_WIDTH]
    v = layernorm(zg[..., GMLP_WIDTH:], v_norm_g, v_norm_b)
    return q_lat, q_rope, c_kv, k_r, u, v


def mla_prompt_attend(q_lat, q_rope, c_kv, k_r):
    B, S = q_lat.shape[:2]
    nb = S // Q_BLOCK
    ql = q_lat.reshape(B, nb, Q_BLOCK, N_HEADS, KV_RANK).transpose(1, 0, 2, 3, 4)
    qr = q_rope.reshape(B, nb, Q_BLOCK, N_HEADS, QK_ROPE).transpose(1, 0, 2, 3, 4)
    kpos = jnp.arange(S)

    def block(args):
        i, qlb, qrb = args
        s = (jnp.einsum('bqhr,bkr->bhqk', qlb, c_kv)
             + jnp.einsum('bqhp,bkp->bhqk', qrb, k_r)).astype(jnp.float32)
        qpos = i * Q_BLOCK + jnp.arange(Q_BLOCK)
        s = jnp.where(kpos[None, :] <= qpos[:, None], s, -jnp.inf)
        p = jax.nn.softmax(s, axis=-1).astype(c_kv.dtype)
        return jnp.einsum('bhqk,bkr->bqhr', p, c_kv)

    o = lax.map(block, (jnp.arange(nb), ql, qr))
    return o.transpose(1, 0, 2, 3, 4).reshape(B, S, N_HEADS, KV_RANK)


def mla_sample_attend(q_lat, q_rope, c_kv, k_r, cache_ckv, cache_krope, page_table, layer):
    DB, T = q_lat.shape[:2]
    past_c = cache_ckv[layer, page_table].reshape(DB, -1, KV_RANK)
    past_r = cache_krope[layer, page_table].reshape(DB, -1, QK_ROPE)
    P = past_c.shape[1]
    s_past = (jnp.einsum('bqhr,bkr->bhqk', q_lat, past_c)
              + jnp.einsum('bqhp,bkp->bhqk', q_rope, past_r)).astype(jnp.float32)
    s_new = (jnp.einsum('bqhr,bkr->bhqk', q_lat, c_kv)
             + jnp.einsum('bqhp,bkp->bhqk', q_rope, k_r)).astype(jnp.float32)
    causal = jnp.tril(jnp.ones((T, T), dtype=bool))
    s_new = jnp.where(causal, s_new, -jnp.inf)
    p = jax.nn.softmax(jnp.concatenate([s_past, s_new], axis=-1), axis=-1).astype(c_kv.dtype)
    return (jnp.einsum('bhqk,bkr->bqhr', p[..., :P], past_c)
            + jnp.einsum('bhqk,bkr->bqhr', p[..., P:], c_kv))


def spatial_gate(u, v, w_s, b_s):
    B, S, _ = v.shape
    L = min(S, CHUNK)
    nc = S // L
    w = jnp.where(jnp.tril(jnp.ones((L, L), dtype=bool)), w_s[:, :L, :L], 0)
    vc = v.reshape(B, nc, L, N_GROUPS, GROUP_DIM)
    mix = jnp.einsum('gij,bcjgd->bcigd', w, vc) + b_s[:, :L].T[None, None, :, :, None]
    return u * mix.reshape(B, S, GMLP_WIDTH)


def trunk_layer(x, pos, attend, norm1_g, w_in, q_norm_g, w_q_up, kv_norm_g, w_uk, w_uv,
                v_norm_g, v_norm_b, w_spatial, b_spatial, out_norm_mla_g, out_norm_gmlp_g,
                w_out, norm2_g, w_gate, w_up, w_down):
    B, S, _ = x.shape
    h = rmsnorm(x, norm1_g)
    q_lat, q_rope, c_kv, k_r, u, v = mixer_front(h, pos, w_in, q_norm_g, w_q_up, kv_norm_g,
                                                 w_uk, v_norm_g, v_norm_b)
    o_lat = attend(q_lat, q_rope, c_kv, k_r)
    a = jnp.einsum('bshr,rhv->bshv', o_lat, w_uv).reshape(B, S, MLA_WIDTH)
    g = spatial_gate(u, v, w_spatial, b_spatial)
    merged = jnp.concatenate([rmsnorm(a, out_norm_mla_g), rmsnorm(g, out_norm_gmlp_g)], axis=-1)
    x = x + merged @ w_out
    h2 = rmsnorm(x, norm2_g)
    x = x + (jax.nn.silu(h2 @ w_gate) * (h2 @ w_up)) @ w_down
    return x, c_kv, k_r, v


def setup_inputs(seed: int = 0) -> dict:
    key = jax.random.key(seed)
    ks = iter(jax.random.split(key, 32))
    f32 = jnp.float32

    def nrm(shape, scale):
        return jax.random.normal(next(ks), shape, f32) * scale

    def gain(shape):
        return 1.0 + 0.05 * jax.random.normal(next(ks), shape, f32)

    n_pages = PAST_LEN // PAGE_SIZE
    n_used = DEC_BATCH * n_pages
    n_pool = n_used + (n_used + 3) // 4
    page_table = jax.random.permutation(next(ks), n_pool)[:n_used].reshape(DEC_BATCH, n_pages).astype(jnp.int32)

    return {
        "x_prompt": nrm((BATCH, SEQ, D_MODEL), 1.0),
        "x_sample": nrm((DEC_BATCH, DEC_SEQ, D_MODEL), 1.0),
        "cache_ckv": nrm((DEPTH, n_pool, PAGE_SIZE, KV_RANK), 1.0),
        "cache_krope": nrm((DEPTH, n_pool, PAGE_SIZE, QK_ROPE), 1.0),
        "page_table": page_table,
        "norm1_g": gain((DEPTH, D_MODEL)),
        "w_in": nrm((DEPTH, D_MODEL, IN_COLS), D_MODEL ** -0.5),
        "q_norm_g": gain((DEPTH, Q_RANK)),
        "w_q_up": nrm((DEPTH, Q_RANK, N_HEADS * (QK_NOPE + QK_ROPE)), Q_RANK ** -0.5),
        "kv_norm_g": gain((DEPTH, KV_RANK)),
        "w_uk": nrm((DEPTH, KV_RANK, N_HEADS, QK_NOPE), KV_RANK ** -0.5),
        "w_uv": nrm((DEPTH, KV_RANK, N_HEADS, V_HEAD), KV_RANK ** -0.5),
        "v_norm_g": gain((DEPTH, GMLP_WIDTH)),
        "v_norm_b": nrm((DEPTH, GMLP_WIDTH), 0.02),
        "w_spatial": nrm((DEPTH, N_GROUPS, CHUNK, CHUNK), CHUNK ** -0.5),
        "b_spatial": 1.0 + nrm((DEPTH, N_GROUPS, CHUNK), 0.1),
        "out_norm_mla_g": gain((DEPTH, MLA_WIDTH)),
        "out_norm_gmlp_g": gain((DEPTH, GMLP_WIDTH)),
        "w_out": nrm((DEPTH, MIX_WIDTH, D_MODEL), MIX_WIDTH ** -0.5),
        "norm2_g": gain((DEPTH, D_MODEL)),
        "w_gate": nrm((DEPTH, D_MODEL, D_FF), D_MODEL ** -0.5),
        "w_up": nrm((DEPTH, D_MODEL, D_FF), D_MODEL ** -0.5),
        "w_down": nrm((DEPTH, D_FF, D_MODEL), D_FF ** -0.5),
        "final_norm_g": gain((D_MODEL,)),
    }


def reference(x_prompt, x_sample, cache_ckv, cache_krope, page_table, norm1_g, w_in, q_norm_g,
              w_q_up, kv_norm_g, w_uk, w_uv, v_norm_g, v_norm_b, w_spatial, b_spatial,
              out_norm_mla_g, out_norm_gmlp_g, w_out, norm2_g, w_gate, w_up, w_down, final_norm_g):
    S = x_prompt.shape[1]
    T = x_sample.shape[1]
    pos_p = jnp.arange(S)
    pos_s = PAST_LEN + jnp.arange(T)
    xp, xs = x_prompt, x_sample
    ckv_p, kr_p, ckv_s, kr_s, v_s = [], [], [], [], []
    for l in range(DEPTH):
        lw = (norm1_g[l], w_in[l], q_norm_g[l], w_q_up[l], kv_norm_g[l], w_uk[l], w_uv[l],
              v_norm_g[l], v_norm_b[l], w_spatial[l], b_spatial[l], out_norm_mla_g[l],
              out_norm_gmlp_g[l], w_out[l], norm2_g[l], w_gate[l], w_up[l], w_down[l])
        xp, c1, r1, _ = trunk_layer(xp, pos_p, mla_prompt_attend, *lw)
        attend_s = functools.partial(mla_sample_attend, cache_ckv=cache_ckv, cache_krope=cache_krope,
                                     page_table=page_table, layer=l)
        xs, c2, r2, v2 = trunk_layer(xs, pos_s, attend_s, *lw)
        ckv_p.append(c1)
        kr_p.append(r1)
        ckv_s.append(c2)
        kr_s.append(r2)
        v_s.append(v2)
    y_prompt = rmsnorm(xp, final_norm_g)
    y_sample = rmsnorm(xs, final_norm_g)
    ckv_prompt = jnp.stack(ckv_p)
    krope_prompt = jnp.stack(kr_p)
    ckv_sample = jnp.stack(ckv_s)
    krope_sample = jnp.stack(kr_s)
    gmlp_v_sample = jnp.stack(v_s)
    return (y_prompt, y_sample, ckv_prompt, krope_prompt, ckv_sample, krope_sample, gmlp_v_sample)
```

```python
import functools
import math

import jax
import jax.numpy as jnp
from jax import lax
from jax.experimental import pallas as pl
from jax.experimental.pallas import tpu as pltpu

LANES = 128
VMEM_LIMIT = 56 * 1024 * 1024

N_HEADS = 8
QK_NOPE = 64
QK_ROPE = 32
ROPE_HALF = QK_ROPE // 2
Q_RANK = 384
KV_RANK = 256
V_HEAD = 64
N_GROUPS = 8
GROUP_DIM = 64
GMLP_WIDTH = N_GROUPS * GROUP_DIM
MLA_WIDTH = N_HEADS * V_HEAD
CHUNK = 128
PAGE_SIZE = 128
ROPE_BASE = 10000.0
EPS = 1e-6
QK_SCALE = (QK_NOPE + QK_ROPE) ** -0.5

COL_Q = 0
COL_KV = COL_Q + Q_RANK
COL_U = COL_KV + KV_RANK
COL_V = COL_U + GMLP_WIDTH
COL_KR = COL_V + GMLP_WIDTH
IN_COLS_PAD = COL_KR + LANES
PK_R1 = QK_NOPE
PK_R2 = QK_NOPE + ROPE_HALF
PK_END = QK_NOPE + QK_ROPE

BF16 = jnp.bfloat16
F32 = jnp.float32


def _rms(x, g):
    return x * lax.rsqrt(jnp.mean(x * x, axis=-1, keepdims=True) + EPS) * g


def _dot(a, b):
    return jnp.dot(a, b, preferred_element_type=F32)


def _dot_nt(a, b):
    return lax.dot_general(a, b, (((1,), (1,)), ((), ())), preferred_element_type=F32)


def _rope_packed(t, c_mul, s_up, s_down):
    return (t * c_mul
            + pltpu.roll(t, LANES - ROPE_HALF, axis=1) * s_up
            + pltpu.roll(t, ROPE_HALF, axis=1) * s_down)


def _front_kernel(absorbed, x_ref, cos_ref, sin_ref, g1_ref, w_in_ref, gq_ref, wq_ref, gkv_ref,
                  wuk_ref, vg_ref, vb_ref, *out_refs):
    if absorbed:
        qlat_ref, qrope_ref, ckv_ref, kr_ref, u_ref, v_ref = out_refs
    else:
        qpk_ref, kt_ref, ckv_ref, ckvb_ref, kr_ref, u_ref, v_ref = out_refs
    x = x_ref[...]
    h = _rms(x, g1_ref[...]).astype(BF16)
    z = _dot(h, w_in_ref[...])

    cos_t = cos_ref[...]
    sin_t = sin_ref[...]
    lane = lax.broadcasted_iota(jnp.int32, cos_t.shape, 1)

    ckv = _rms(z[:, COL_KV:COL_U], gkv_ref[...])
    ckv_ref[...] = ckv
    kr_blk = z[:, COL_KR:COL_KR + LANES]
    k_c = jnp.where(lane < QK_ROPE, cos_t, 0.0)
    k_up = jnp.where(lane < ROPE_HALF, -sin_t, 0.0)
    k_dn = jnp.where((lane >= ROPE_HALF) & (lane < QK_ROPE), sin_t, 0.0)
    kr = _rope_packed(kr_blk, k_c, k_up, k_dn)
    kr_ref[...] = kr[:, :QK_ROPE]

    cqn = _rms(z[:, COL_Q:COL_KV], gq_ref[...]).astype(BF16)
    qp = _dot(cqn, wq_ref[...])
    q_c = jnp.where(lane < PK_R1, QK_SCALE, jnp.where(lane < PK_END, QK_SCALE * cos_t, 0.0))
    q_up = jnp.where((lane >= PK_R1) & (lane < PK_R2), -QK_SCALE * sin_t, 0.0)
    q_dn = jnp.where((lane >= PK_R2) & (lane < PK_END), QK_SCALE * sin_t, 0.0)
    for hd in range(N_HEADS):
        qh = _rope_packed(qp[:, hd * LANES:(hd + 1) * LANES], q_c, q_up, q_dn)
        if absorbed:
            qlat_ref[hd] = _dot(qh.astype(BF16), wuk_ref[hd])
            qrope_ref[hd] = qh[:, PK_R1:PK_END]
        else:
            qpk_ref[hd] = qh.astype(BF16)

    if not absorbed:
        ckv_b = ckv.astype(BF16)
        ckvb_ref[...] = ckv_b
        knt = _dot_nt(wuk_ref[...], ckv_b)
        krt = jnp.transpose(kr)[:QK_ROPE].astype(BF16)
        zpad = jnp.zeros((LANES - PK_END, krt.shape[1]), BF16)
        for hd in range(N_HEADS):
            kt_ref[hd, 0:QK_NOPE, :] = knt[hd * QK_NOPE:(hd + 1) * QK_NOPE].astype(BF16)
            kt_ref[hd, QK_NOPE:PK_END, :] = krt
            kt_ref[hd, PK_END:LANES, :] = zpad

    zg = z[:, COL_U:COL_KR]
    zg = 0.5 * zg * (1.0 + lax.erf(zg * math.sqrt(0.5)))
    u_ref[...] = zg[:, :GMLP_WIDTH]
    vv = zg[:, GMLP_WIDTH:]
    mu = jnp.mean(vv, axis=-1, keepdims=True)
    vc = vv - mu
    v_ref[...] = vc * lax.rsqrt(jnp.mean(vc * vc, axis=-1, keepdims=True) + EPS) * vg_ref[...] + vb_ref[...]


def _const_spec(shape):
    nd = len(shape)
    return pl.BlockSpec(shape, lambda i, _nd=nd: (0,) * _nd)


def _front(x, cos_t, sin_t, wts, *, absorbed, tm, seq=None):
    n_tok, d_model = x.shape
    grid = (n_tok // tm,)
    row = lambda w: pl.BlockSpec((tm, w), lambda i: (i, 0))
    head = lambda w, dt: (jax.ShapeDtypeStruct((N_HEADS, n_tok, w), dt),
                          pl.BlockSpec((N_HEADS, tm, w), lambda i: (0, i, 0)))
    flat = lambda w, dt: (jax.ShapeDtypeStruct((n_tok, w), dt), row(w))
    if absorbed:
        outs = [head(KV_RANK, F32), head(QK_ROPE, F32), flat(KV_RANK, F32), flat(QK_ROPE, F32),
                flat(GMLP_WIDTH, F32), flat(GMLP_WIDTH, F32)]
    else:
        spb = seq // tm
        kt = (jax.ShapeDtypeStruct((n_tok // seq, N_HEADS, LANES, seq), BF16),
              pl.BlockSpec((None, N_HEADS, LANES, tm), lambda i: (i // spb, 0, 0, i % spb)))
        outs = [head(LANES, BF16), kt, flat(KV_RANK, F32), flat(KV_RANK, BF16), flat(QK_ROPE, F32),
                flat(GMLP_WIDTH, F32), flat(GMLP_WIDTH, F32)]
    ins = [x, cos_t, sin_t] + list(wts)
    in_specs = [row(d_model), row(LANES), row(LANES)] + [_const_spec(w.shape) for w in wts]
    return pl.pallas_call(
        functools.partial(_front_kernel, absorbed),
        grid=grid,
        in_specs=in_specs,
        out_specs=[o[1] for o in outs],
        out_shape=[o[0] for o in outs],
        compiler_params=pltpu.CompilerParams(dimension_semantics=("parallel",),
                                             vmem_limit_bytes=VMEM_LIMIT),
        name="front_absorbed" if absorbed else "front_prompt",
    )(*ins)


def _softmax_step(s, v_b, m_ref, l_ref, acc_ref, idx):
    m_prev = m_ref[idx]
    m_new = jnp.maximum(m_prev, jnp.max(s, axis=-1, keepdims=True))
    alpha = jnp.exp(m_prev - m_new)
    p = jnp.exp(s - m_new)
    l_ref[idx] = alpha * l_ref[idx] + jnp.sum(p, axis=-1, keepdims=True)
    acc_ref[idx] = alpha * acc_ref[idx] + _dot(p.astype(BF16), v_b)
    m_ref[idx] = m_new


def _prompt_attn_kernel(qi_ref, ki_ref, q_ref, kt_ref, v_ref, o_ref, m_ref, l_ref, acc_ref):
    p_id = pl.program_id(1)
    qi = qi_ref[p_id]
    ki = ki_ref[p_id]
    blk = q_ref.shape[1]

    @pl.when(ki == 0)
    def _():
        m_ref[...] = jnp.full(m_ref.shape, -jnp.inf, F32)
        l_ref[...] = jnp.zeros(l_ref.shape, F32)
        acc_ref[...] = jnp.zeros(acc_ref.shape, F32)

    def step(diagonal):
        v_b = v_ref[...]
        if diagonal:
            rows = lax.broadcasted_iota(jnp.int32, (blk, blk), 0)
            cols = lax.broadcasted_iota(jnp.int32, (blk, blk), 1)
            keep = cols <= rows
        for hd in range(N_HEADS):
            s = _dot(q_ref[hd], kt_ref[hd])
            if diagonal:
                s = jnp.where(keep, s, -jnp.inf)
            _softmax_step(s, v_b, m_ref, l_ref, acc_ref, hd)

    @pl.when(ki < qi)
    def _():
        step(False)

    @pl.when(ki == qi)
    def _():
        step(True)
        for hd in range(N_HEADS):
            o_ref[hd] = (acc_ref[hd] * (1.0 / l_ref[hd])).astype(o_ref.dtype)


def _prompt_attend(q_pk, kt, ckv_b, *, batch, seq, blk):
    nq = seq // blk
    pairs = [(q, k) for q in range(nq) for k in range(q + 1)]
    qi = jnp.asarray([p[0] for p in pairs], jnp.int32)
    ki = jnp.asarray([p[1] for p in pairs], jnp.int32)
    n_tok = batch * seq
    grid_spec = pltpu.PrefetchScalarGridSpec(
        num_scalar_prefetch=2,
        grid=(batch, len(pairs)),
        in_specs=[
            pl.BlockSpec((N_HEADS, blk, LANES), lambda b, p, qi, ki: (0, b * nq + qi[p], 0)),
            pl.BlockSpec((None, N_HEADS, LANES, blk), lambda b, p, qi, ki: (b, 0, 0, ki[p])),
            pl.BlockSpec((blk, KV_RANK), lambda b, p, qi, ki: (b * nq + ki[p], 0)),
        ],
        out_specs=pl.BlockSpec((N_HEADS, blk, KV_RANK), lambda b, p, qi, ki: (0, b * nq + qi[p], 0)),
        scratch_shapes=[pltpu.VMEM((N_HEADS, blk, 1), F32), pltpu.VMEM((N_HEADS, blk, 1), F32),
                        pltpu.VMEM((N_HEADS, blk, KV_RANK), F32)],
    )
    return pl.pallas_call(
        _prompt_attn_kernel,
        grid_spec=grid_spec,
        out_shape=jax.ShapeDtypeStruct((N_HEADS, n_tok, KV_RANK), BF16),
        compiler_params=pltpu.CompilerParams(dimension_semantics=("parallel", "arbitrary"),
                                             vmem_limit_bytes=VMEM_LIMIT),
        name="prompt_attend",
    )(qi, ki, q_pk, kt, ckv_b)


def _decode_attn_kernel(n_pg, pt_ref, ql_ref, qr_ref, cn_ref, rn_ref, *rest):
    del pt_ref
    page_refs = rest[:n_pg]
    rope_refs = rest[n_pg:2 * n_pg]
    o_ref, m_ref, l_ref, acc_ref = rest[2 * n_pg:]
    j = pl.program_id(1)
    rows = N_HEADS * ql_ref.shape[1]
    ql = ql_ref[...].reshape(rows, KV_RANK).astype(BF16)
    qr = qr_ref[...].reshape(rows, QK_ROPE).astype(BF16)

    @pl.when(j == 0)
    def _():
        m_ref[...] = jnp.full(m_ref.shape, -jnp.inf, F32)
        l_ref[...] = jnp.zeros(l_ref.shape, F32)
        acc_ref[...] = jnp.zeros(acc_ref.shape, F32)

    kb = jnp.concatenate([r[...].astype(BF16) for r in page_refs], axis=0)
    rb = jnp.concatenate([r[...].astype(BF16) for r in rope_refs], axis=0)
    s = _dot_nt(ql, kb) + _dot_nt(qr, rb)
    _softmax_step(s, kb, m_ref, l_ref, acc_ref, Ellipsis)

    @pl.when(j == pl.num_programs(1) - 1)
    def _():
        t_new = cn_ref.shape[0]
        cb = cn_ref[...].astype(BF16)
        s_new = _dot_nt(ql, cb) + _dot_nt(qr, rn_ref[...].astype(BF16))
        r_pos = lax.broadcasted_iota(jnp.int32, s_new.shape, 0) % t_new
        c_pos = lax.broadcasted_iota(jnp.int32, s_new.shape, 1)
        s_new = jnp.where(c_pos <= r_pos, s_new, -jnp.inf)
        _softmax_step(s_new, cb, m_ref, l_ref, acc_ref, Ellipsis)
        o = acc_ref[...] * (1.0 / l_ref[...])
        o_ref[...] = o.reshape(o_ref.shape)


def _decode_attend(q_lat, q_rope, ckv_new, kr_new, cache_ckv, cache_krope, page_table, *, t_new, n_pg):
    dec_batch, n_pages = page_table.shape
    n_tok = dec_batch * t_new
    steps = n_pages // n_pg
    pt_flat = page_table.reshape(-1)

    def page_map(i):
        return lambda b, j, pt: (0, pt[b * n_pages + j * n_pg + i], 0, 0)

    in_specs = [
        pl.BlockSpec((N_HEADS, t_new, KV_RANK), lambda b, j, pt: (0, b, 0)),
        pl.BlockSpec((N_HEADS, t_new, QK_ROPE), lambda b, j, pt: (0, b, 0)),
        pl.BlockSpec((t_new, KV_RANK), lambda b, j, pt: (b, 0)),
        pl.BlockSpec((t_new, QK_ROPE), lambda b, j, pt: (b, 0)),
    ]
    in_specs += [pl.BlockSpec((None, None, PAGE_SIZE, KV_RANK), page_map(i)) for i in range(n_pg)]
    in_specs += [pl.BlockSpec((None, None, PAGE_SIZE, QK_ROPE), page_map(i)) for i in range(n_pg)]
    rows = N_HEADS * t_new
    grid_spec = pltpu.PrefetchScalarGridSpec(
        num_scalar_prefetch=1,
        grid=(dec_batch, steps),
        in_specs=in_specs,
        out_specs=pl.BlockSpec((N_HEADS, t_new, KV_RANK), lambda b, j, pt: (0, b, 0)),
        scratch_shapes=[pltpu.VMEM((rows, 1), F32), pltpu.VMEM((rows, 1), F32),
                        pltpu.VMEM((rows, KV_RANK), F32)],
    )
    return pl.pallas_call(
        functools.partial(_decode_attn_kernel, n_pg),
        grid_spec=grid_spec,
        out_shape=jax.ShapeDtypeStruct((N_HEADS, n_tok, KV_RANK), F32),
        compiler_params=pltpu.CompilerParams(dimension_semantics=("parallel", "arbitrary"),
                                             vmem_limit_bytes=VMEM_LIMIT),
        name="decode_attend",
    )(pt_flat, q_lat, q_rope, ckv_new, kr_new, *([cache_ckv] * n_pg), *([cache_krope] * n_pg))


def _back_kernel(o_ref, u_ref, v_ref, x_ref, wuv_ref, wmix_ref, bmix_ref, ga_ref, gg_ref, wout_ref,
                 g2_ref, wgate_ref, wup_ref, wdown_ref, gf_ref, y_ref):
    tm = x_ref.shape[0]
    o_flat = jnp.concatenate([o_ref[hd].astype(BF16) for hd in range(N_HEADS)], axis=1)
    a = _dot(o_flat, wuv_ref[...])
    an = _rms(a, ga_ref[...])

    lane = lax.broadcasted_iota(jnp.int32, (CHUNK, GMLP_WIDTH), 1)
    wmix = wmix_ref[...]
    mixes = []
    for c in range(tm // CHUNK):
        vck = v_ref[c * CHUNK:(c + 1) * CHUNK, :].astype(BF16)
        bd = jnp.concatenate(
            [jnp.where((lane >= g * GROUP_DIM) & (lane < (g + 1) * GROUP_DIM), vck, jnp.zeros_like(vck))
             for g in range(N_GROUPS)], axis=0)
        mixes.append(_dot(wmix, bd) + bmix_ref[...])
    mix = mixes[0] if len(mixes) == 1 else jnp.concatenate(mixes, axis=0)
    gn = _rms(u_ref[...] * mix, gg_ref[...])

    merged = jnp.concatenate([an, gn], axis=1).astype(BF16)
    x1 = x_ref[...] + _dot(merged, wout_ref[...])
    h2 = _rms(x1, g2_ref[...]).astype(BF16)
    gate = _dot(h2, wgate_ref[...])
    up = _dot(h2, wup_ref[...])
    act = (gate * (1.0 / (1.0 + jnp.exp(-gate))) * up).astype(BF16)
    x2 = x1 + _dot(act, wdown_ref[...])
    y_ref[...] = _rms(x2, gf_ref[...])


def _back(o_lat, u, v, x, wts, *, tm, name):
    n_tok, d_model = x.shape
    row = lambda w: pl.BlockSpec((tm, w), lambda i: (i, 0))
    in_specs = [pl.BlockSpec((N_HEADS, tm, KV_RANK), lambda i: (0, i, 0)), row(GMLP_WIDTH), row(GMLP_WIDTH),
                row(d_model)]
    in_specs += [pl.BlockSpec(w.shape, lambda i, _nd=w.ndim: (0,) * _nd, pipeline_mode=pl.Buffered(1))
                 for w in wts]
    return pl.pallas_call(
        _back_kernel,
        grid=(n_tok // tm,),
        in_specs=in_specs,
        out_specs=row(d_model),
        out_shape=jax.ShapeDtypeStruct((n_tok, d_model), F32),
        compiler_params=pltpu.CompilerParams(dimension_semantics=("parallel",),
                                             vmem_limit_bytes=VMEM_LIMIT),
        name=name,
    )(o_lat, u, v, x, *wts)


def _rope_tables(pos):
    inv = ROPE_BASE ** (-jnp.arange(ROPE_HALF, dtype=F32) / ROPE_HALF)
    ang = pos.astype(F32)[:, None] * inv[None, :]
    reps = LANES // ROPE_HALF
    return jnp.tile(jnp.cos(ang), (1, reps)), jnp.tile(jnp.sin(ang), (1, reps))


def kernel(x_prompt, x_sample, cache_ckv, cache_krope, page_table, norm1_g, w_in, q_norm_g, w_q_up, kv_norm_g, w_uk, w_uv, v_norm_g, v_norm_b, w_spatial, b_spatial, out_norm_mla_g, out_norm_gmlp_g, w_out, norm2_g, w_gate, w_up, w_down, final_norm_g):
    batch, seq, d_model = x_prompt.shape
    dec_batch, t_new, _ = x_sample.shape
    depth = w_in.shape[0]
    past_len = page_table.shape[1] * PAGE_SIZE
    assert depth == 1 and seq % CHUNK == 0 and t_new <= CHUNK and CHUNK % t_new == 0

    xp = x_prompt.reshape(batch * seq, d_model)
    xs = x_sample.reshape(dec_batch * t_new, d_model)
    cos_p, sin_p = _rope_tables(jnp.arange(seq))
    cos_p, sin_p = jnp.tile(cos_p, (batch, 1)), jnp.tile(sin_p, (batch, 1))
    cos_s, sin_s = _rope_tables(past_len + jnp.arange(t_new))
    cos_s, sin_s = jnp.tile(cos_s, (dec_batch, 1)), jnp.tile(sin_s, (dec_batch, 1))

    l = 0
    row2 = lambda g: g.reshape(1, -1)
    wi = w_in[l]
    off_kr = Q_RANK + KV_RANK
    off_g = off_kr + QK_ROPE
    w_in_p = jnp.concatenate(
        [wi[:, :off_kr], wi[:, off_g:], wi[:, off_kr:off_g],
         jnp.zeros((d_model, IN_COLS_PAD - COL_KR - QK_ROPE), wi.dtype)], axis=1).astype(BF16)
    wq = w_q_up[l].reshape(Q_RANK, N_HEADS, QK_NOPE + QK_ROPE)
    wq_p = jnp.concatenate([wq, jnp.zeros((Q_RANK, N_HEADS, LANES - PK_END), wq.dtype)], axis=2)
    wq_p = wq_p.reshape(Q_RANK, N_HEADS * LANES).astype(BF16)
    wuk_t = w_uk[l].reshape(KV_RANK, N_HEADS * QK_NOPE).T.astype(BF16)
    wuk_h = jnp.transpose(w_uk[l], (1, 2, 0))
    wuk_h = jnp.concatenate([wuk_h, jnp.zeros((N_HEADS, LANES - QK_NOPE, KV_RANK), wuk_h.dtype)],
                            axis=1).astype(BF16)
    front_common = (row2(norm1_g[l]), w_in_p, row2(q_norm_g[l]), wq_p, row2(kv_norm_g[l]))
    front_tail = (row2(v_norm_g[l]), row2(v_norm_b[l]))

    eye_h = jnp.eye(N_HEADS, dtype=w_uv.dtype)
    wuv_bd = jnp.einsum('rhv,hg->hrgv', w_uv[l], eye_h).reshape(N_HEADS * KV_RANK, MLA_WIDTH).astype(BF16)
    tril = jnp.tril(jnp.ones((CHUNK, CHUNK), dtype=bool))
    wmix_p = jnp.where(tril, w_spatial[l], 0)
    bmix_p = jnp.repeat(b_spatial[l].T, GROUP_DIM, axis=1)
    reps = CHUNK // t_new
    tril_s = jnp.tril(jnp.ones((t_new, t_new), dtype=bool))
    w_small = jnp.where(tril_s, w_spatial[l][:, :t_new, :t_new], 0)
    wmix_s = jnp.einsum('ab,gij->gaibj', jnp.eye(reps, dtype=w_small.dtype), w_small)
    wmix_s = wmix_s.reshape(N_GROUPS, CHUNK, CHUNK)
    bmix_s = jnp.tile(jnp.repeat(b_spatial[l][:, :t_new].T, GROUP_DIM, axis=1), (reps, 1))
    cat_groups = lambda w: jnp.transpose(w, (1, 0, 2)).reshape(CHUNK, N_GROUPS * CHUNK).astype(BF16)
    back_tail = (row2(out_norm_mla_g[l]), row2(out_norm_gmlp_g[l]), w_out[l].astype(BF16), row2(norm2_g[l]),
                 w_gate[l].astype(BF16), w_up[l].astype(BF16), w_down[l].astype(BF16), row2(final_norm_g))

    q_pk, kt, ckv_p, ckv_pb, kr_p, u_p, v_p = _front(
        xp, cos_p, sin_p, front_common + (wuk_t,) + front_tail, absorbed=False, tm=512, seq=seq)
    o_p = _prompt_attend(q_pk, kt, ckv_pb, batch=batch, seq=seq, blk=512)
    y_p = _back(o_p, u_p, v_p, xp, (wuv_bd, cat_groups(wmix_p), bmix_p) + back_tail, tm=256, name="back_prompt")

    q_lat, q_rope, ckv_s, kr_s, u_s, v_s = _front(
        xs, cos_s, sin_s, front_common + (wuk_h,) + front_tail, absorbed=True, tm=256)
    o_s = _decode_attend(q_lat, q_rope, ckv_s, kr_s, cache_ckv, cache_krope, page_table, t_new=t_new, n_pg=8)
    y_s = _back(o_s, u_s, v_s, xs, (wuv_bd, cat_groups(wmix_s), bmix_s) + back_tail, tm=256, name="back_decode")

    return (y_p.reshape(batch, seq, d_model),
            y_s.reshape(dec_batch, t_new, d_model),
            ckv_p.reshape(depth, batch, seq, KV_RANK),
            kr_p.reshape(depth, batch, seq, QK_ROPE),
            ckv_s.reshape(depth, dec_batch, t_new, KV_RANK),
            kr_s.reshape(depth, dec_batch, t_new, QK_ROPE),
            v_s.reshape(depth, dec_batch, t_new, GMLP_WIDTH))
```

```python
import functools
import math

import jax
import jax.numpy as jnp
from jax import lax
from jax.experimental import pallas as pl
from jax.experimental.pallas import tpu as pltpu

LANES = 128
MXU_DIM = 256
VMEM_LIMIT = 56 * 1024 * 1024

N_HEADS = 8
QK_NOPE = 64
QK_ROPE = 32
ROPE_HALF = QK_ROPE // 2
Q_RANK = 384
KV_RANK = 256
V_HEAD = 64
N_GROUPS = 8
GROUP_DIM = 64
GMLP_WIDTH = N_GROUPS * GROUP_DIM
MLA_WIDTH = N_HEADS * V_HEAD
CHUNK = 128
PAGE_SIZE = 128
ROPE_BASE = 10000.0
EPS = 1e-6
Q_SCALE = (QK_NOPE + QK_ROPE) ** -0.5 * math.log2(math.e)

COL_Q = 0
COL_KV = COL_Q + Q_RANK
COL_U = COL_KV + KV_RANK
COL_V = COL_U + GMLP_WIDTH
COL_KR = COL_V + GMLP_WIDTH
IN_COLS_PAD = COL_KR + LANES
PK_R1 = QK_NOPE
PK_R2 = QK_NOPE + ROPE_HALF
PK_END = QK_NOPE + QK_ROPE

BF16 = jnp.bfloat16
F32 = jnp.float32


def _rms(x, g):
    return x * lax.rsqrt(jnp.mean(x * x, axis=-1, keepdims=True) + EPS) * g


def _dot(a, b):
    return jnp.dot(a, b, preferred_element_type=F32)


def _dot_nt(a, b):
    return lax.dot_general(a, b, (((1,), (1,)), ((), ())), preferred_element_type=F32)


def _rope_packed(t, c_mul, s_up, s_down):
    return (t * c_mul
            + pltpu.roll(t, LANES - ROPE_HALF, axis=1) * s_up
            + pltpu.roll(t, ROPE_HALF, axis=1) * s_down)


def _front_kernel(absorbed, *refs):
    if absorbed:
        (x_ref, cos_ref, sin_ref, g1_ref, w_in_ref, gq_ref, wq_ref, gkv_ref, wuk_ref, vg_ref, vb_ref,
         qlat_ref, qrope_ref, ckv_ref, kr_ref, u_ref, v_ref) = refs
    else:
        (x_ref, cos_ref, sin_ref, cost_ref, sint_ref, g1_ref, w_in_ref, gq_ref, wq_ref, gkv_ref, wuk_ref,
         vg_ref, vb_ref, qt_ref, kpk_ref, ckv_ref, ckvt_ref, kr_ref, u_ref, v_ref) = refs
    x = x_ref[...]
    h = _rms(x, g1_ref[...]).astype(BF16)
    z = _dot(h, w_in_ref[...])

    cos_t = cos_ref[...]
    sin_t = sin_ref[...]
    lane = lax.broadcasted_iota(jnp.int32, cos_t.shape, 1)
    in_r1 = (lane >= PK_R1) & (lane < PK_R2)
    in_r2 = (lane >= PK_R2) & (lane < PK_END)

    ckv = _rms(z[:, COL_KV:COL_U], gkv_ref[...])
    ckv_ref[...] = ckv
    k_c = jnp.where(in_r1 | in_r2, cos_t, 0.0)
    k_up = jnp.where(in_r1, -sin_t, 0.0)
    k_dn = jnp.where(in_r2, sin_t, 0.0)
    kr = _rope_packed(z[:, COL_KR:COL_KR + LANES], k_c, k_up, k_dn)
    kr_ref[...] = kr[:, PK_R1:PK_END]

    cqn = _rms(z[:, COL_Q:COL_KV], gq_ref[...]).astype(BF16)
    if absorbed:
        qp = _dot(cqn, wq_ref[...])
        q_c = jnp.where(lane < PK_R1, Q_SCALE, jnp.where(lane < PK_END, Q_SCALE * cos_t, 0.0))
        q_up = jnp.where(in_r1, -Q_SCALE * sin_t, 0.0)
        q_dn = jnp.where(in_r2, Q_SCALE * sin_t, 0.0)
        for hd in range(N_HEADS):
            qh = _rope_packed(qp[:, hd * LANES:(hd + 1) * LANES], q_c, q_up, q_dn)
            qlat_ref[hd] = _dot(qh.astype(BF16), wuk_ref[hd])
            qrope_ref[hd] = qh[:, PK_R1:PK_END]
    else:
        qpt = _dot_nt(wq_ref[...], cqn)
        c_t = cost_ref[...] * Q_SCALE
        s_t = sint_ref[...] * Q_SCALE
        zpad = jnp.zeros((LANES - PK_END, qpt.shape[1]), F32)
        for hd in range(N_HEADS):
            blk = qpt[hd * LANES:(hd + 1) * LANES]
            x1 = blk[PK_R1:PK_R2]
            x2 = blk[PK_R2:PK_END]
            qt = jnp.concatenate([blk[:PK_R1] * Q_SCALE, x1 * c_t - x2 * s_t, x1 * s_t + x2 * c_t, zpad],
                                 axis=0)
            qt_ref[hd] = qt.astype(BF16)
        ckv_b = ckv.astype(BF16)
        knope = _dot(ckv_b, wuk_ref[...])
        for hd in range(N_HEADS):
            kpk_ref[hd] = (knope[:, hd * LANES:(hd + 1) * LANES] + kr).astype(BF16)
        ckvt_ref[...] = jnp.transpose(ckv).astype(BF16)

    zg = z[:, COL_U:COL_KR]
    zg = 0.5 * zg * (1.0 + lax.erf(zg * math.sqrt(0.5)))
    u_ref[...] = zg[:, :GMLP_WIDTH]
    vv = zg[:, GMLP_WIDTH:]
    mu = jnp.mean(vv, axis=-1, keepdims=True)
    vc = vv - mu
    v_ref[...] = vc * lax.rsqrt(jnp.mean(vc * vc, axis=-1, keepdims=True) + EPS) * vg_ref[...] + vb_ref[...]


def _const_spec(shape):
    nd = len(shape)
    return pl.BlockSpec(shape, lambda i, _nd=nd: (0,) * _nd)


def _front(x, tables, wts, *, absorbed, tm, seq=None):
    n_tok, d_model = x.shape
    grid = (n_tok // tm,)
    row = lambda w: pl.BlockSpec((tm, w), lambda i: (i, 0))
    head = lambda w, dt: (jax.ShapeDtypeStruct((N_HEADS, n_tok, w), dt),
                          pl.BlockSpec((N_HEADS, tm, w), lambda i: (0, i, 0)))
    flat = lambda w, dt: (jax.ShapeDtypeStruct((n_tok, w), dt), row(w))
    table_specs = [row(LANES), row(LANES)]
    if absorbed:
        outs = [head(KV_RANK, F32), head(QK_ROPE, F32), flat(KV_RANK, F32), flat(QK_ROPE, F32),
                flat(GMLP_WIDTH, F32), flat(GMLP_WIDTH, F32)]
    else:
        spb = seq // tm
        nb = n_tok // seq
        table_specs += [pl.BlockSpec((ROPE_HALF, tm), lambda i: (0, i % spb))] * 2
        qt = (jax.ShapeDtypeStruct((nb, N_HEADS, LANES, seq), BF16),
              pl.BlockSpec((None, N_HEADS, LANES, tm), lambda i: (i // spb, 0, 0, i % spb)))
        ckvt = (jax.ShapeDtypeStruct((nb, KV_RANK, seq), BF16),
                pl.BlockSpec((None, KV_RANK, tm), lambda i: (i // spb, 0, i % spb)))
        outs = [qt, head(LANES, BF16), flat(KV_RANK, F32), ckvt, flat(QK_ROPE, F32),
                flat(GMLP_WIDTH, F32), flat(GMLP_WIDTH, F32)]
    ins = [x] + list(tables) + list(wts)
    in_specs = [row(d_model)] + table_specs + [_const_spec(w.shape) for w in wts]
    return pl.pallas_call(
        functools.partial(_front_kernel, absorbed),
        grid=grid,
        in_specs=in_specs,
        out_specs=[o[1] for o in outs],
        out_shape=[o[0] for o in outs],
        compiler_params=pltpu.CompilerParams(dimension_semantics=("parallel",),
                                             vmem_limit_bytes=VMEM_LIMIT),
        name="front_absorbed" if absorbed else "front_prompt",
    )(*ins)


def _prompt_attn_kernel(qi_ref, ki_ref, k_ref, qt_ref, vt_ref, o_ref, m_ref, l_ref, acc_ref):
    p_id = pl.program_id(1)
    qi = qi_ref[p_id]
    ki = ki_ref[p_id]
    blk_k = k_ref.shape[1]
    blk_q = qt_ref.shape[2]
    qw = MXU_DIM

    @pl.when(ki == 0)
    def _():
        m_ref[...] = jnp.full(m_ref.shape, -jnp.inf, F32)
        l_ref[...] = jnp.zeros(l_ref.shape, F32)
        acc_ref[...] = jnp.zeros(acc_ref.shape, F32)

    def step(diagonal):
        vt = vt_ref[...]
        for c in range(blk_q // qw):
            cols = slice(c * qw, (c + 1) * qw)
            if diagonal:
                kpos = lax.broadcasted_iota(jnp.int32, (blk_k, qw), 0)
                qpos = lax.broadcasted_iota(jnp.int32, (blk_k, qw), 1) + c * qw
                keep = kpos <= qpos
            for hd in range(N_HEADS):
                st = _dot(k_ref[hd], qt_ref[hd, :, cols])
                if diagonal:
                    st = jnp.where(keep, st, -jnp.inf)
                m_prev = m_ref[hd, :, cols]
                m_new = jnp.maximum(m_prev, jnp.max(st, axis=0, keepdims=True))
                alpha = jnp.exp2(m_prev - m_new)
                pt = jnp.exp2(st - m_new)
                l_ref[hd, :, cols] = alpha * l_ref[hd, :, cols] + jnp.sum(pt, axis=0, keepdims=True)
                acc_ref[hd, :, cols] = alpha * acc_ref[hd, :, cols] + _dot(vt, pt.astype(BF16))
                m_ref[hd, :, cols] = m_new

    @pl.when(ki < qi)
    def _():
        step(False)

    @pl.when(ki == qi)
    def _():
        step(True)
        for hd in range(N_HEADS):
            ot = acc_ref[hd] * (1.0 / l_ref[hd])
            o_ref[hd] = jnp.transpose(ot).astype(o_ref.dtype)


def _prompt_attend(k_pk, q_t, ckv_t, *, batch, seq, blk):
    nq = seq // blk
    pairs = [(q, k) for q in range(nq) for k in range(q + 1)]
    qi = jnp.asarray([p[0] for p in pairs], jnp.int32)
    ki = jnp.asarray([p[1] for p in pairs], jnp.int32)
    n_tok = batch * seq
    grid_spec = pltpu.PrefetchScalarGridSpec(
        num_scalar_prefetch=2,
        grid=(batch, len(pairs)),
        in_specs=[
            pl.BlockSpec((N_HEADS, blk, LANES), lambda b, p, qi, ki: (0, b * nq + ki[p], 0)),
            pl.BlockSpec((None, N_HEADS, LANES, blk), lambda b, p, qi, ki: (b, 0, 0, qi[p])),
            pl.BlockSpec((None, KV_RANK, blk), lambda b, p, qi, ki: (b, 0, ki[p])),
        ],
        out_specs=pl.BlockSpec((N_HEADS, blk, KV_RANK), lambda b, p, qi, ki: (0, b * nq + qi[p], 0)),
        scratch_shapes=[pltpu.VMEM((N_HEADS, 1, blk), F32), pltpu.VMEM((N_HEADS, 1, blk), F32),
                        pltpu.VMEM((N_HEADS, KV_RANK, blk), F32)],
    )
    return pl.pallas_call(
        _prompt_attn_kernel,
        grid_spec=grid_spec,
        out_shape=jax.ShapeDtypeStruct((N_HEADS, n_tok, KV_RANK), BF16),
        compiler_params=pltpu.CompilerParams(dimension_semantics=("parallel", "arbitrary"),
                                             vmem_limit_bytes=VMEM_LIMIT),
        name="prompt_attend",
    )(qi, ki, k_pk, q_t, ckv_t)


def _decode_softmax_step(s, v_b, m_ref, l_ref, acc_ref):
    m_prev = m_ref[...]
    m_new = jnp.maximum(m_prev, jnp.max(s, axis=-1, keepdims=True))
    alpha = jnp.exp2(m_prev - m_new)
    p = jnp.exp2(s - m_new)
    l_ref[...] = alpha * l_ref[...] + jnp.sum(p, axis=-1, keepdims=True)
    acc_ref[...] = alpha * acc_ref[...] + _dot(p.astype(BF16), v_b)
    m_ref[...] = m_new


def _decode_attn_kernel(n_pg, pt_ref, ql_ref, qr_ref, cn_ref, rn_ref, *rest):
    del pt_ref
    page_refs = rest[:n_pg]
    rope_refs = rest[n_pg:2 * n_pg]
    o_ref, m_ref, l_ref, acc_ref = rest[2 * n_pg:]
    j = pl.program_id(1)
    rows = N_HEADS * ql_ref.shape[1]
    ql = ql_ref[...].reshape(rows, KV_RANK).astype(BF16)
    qr = qr_ref[...].reshape(rows, QK_ROPE).astype(BF16)

    @pl.when(j == 0)
    def _():
        m_ref[...] = jnp.full(m_ref.shape, -jnp.inf, F32)
        l_ref[...] = jnp.zeros(l_ref.shape, F32)
        acc_ref[...] = jnp.zeros(acc_ref.shape, F32)

    kb = jnp.concatenate([r[...].astype(BF16) for r in page_refs], axis=0)
    rbt = jnp.concatenate([r[...].astype(BF16) for r in rope_refs], axis=1)
    s = _dot_nt(ql, kb) + _dot(qr, rbt)
    _decode_softmax_step(s, kb, m_ref, l_ref, acc_ref)

    @pl.when(j == pl.num_programs(1) - 1)
    def _():
        t_new = cn_ref.shape[0]
        cb = cn_ref[...].astype(BF16)
        s_new = _dot_nt(ql, cb) + _dot_nt(qr, rn_ref[...].astype(BF16))
        r_pos = lax.broadcasted_iota(jnp.int32, s_new.shape, 0) % t_new
        c_pos = lax.broadcasted_iota(jnp.int32, s_new.shape, 1)
        s_new = jnp.where(c_pos <= r_pos, s_new, -jnp.inf)
        _decode_softmax_step(s_new, cb, m_ref, l_ref, acc_ref)
        o = acc_ref[...] * (1.0 / l_ref[...])
        o_ref[...] = o.reshape(o_ref.shape)


def _decode_attend(q_lat, q_rope, ckv_new, kr_new, cache_ckv, cache_krope_t, page_table, *, t_new, n_pg):
    dec_batch, n_pages = page_table.shape
    n_tok = dec_batch * t_new
    steps = n_pages // n_pg
    pt_flat = page_table.reshape(-1)

    def page_map(i):
        return lambda b, j, pt: (0, pt[b * n_pages + j * n_pg + i], 0, 0)

    in_specs = [
        pl.BlockSpec((N_HEADS, t_new, KV_RANK), lambda b, j, pt: (0, b, 0)),
        pl.BlockSpec((N_HEADS, t_new, QK_ROPE), lambda b, j, pt: (0, b, 0)),
        pl.BlockSpec((t_new, KV_RANK), lambda b, j, pt: (b, 0)),
        pl.BlockSpec((t_new, QK_ROPE), lambda b, j, pt: (b, 0)),
    ]
    in_specs += [pl.BlockSpec((None, None, PAGE_SIZE, KV_RANK), page_map(i)) for i in range(n_pg)]
    in_specs += [pl.BlockSpec((None, None, QK_ROPE, PAGE_SIZE), page_map(i)) for i in range(n_pg)]
    rows = N_HEADS * t_new
    grid_spec = pltpu.PrefetchScalarGridSpec(
        num_scalar_prefetch=1,
        grid=(dec_batch, steps),
        in_specs=in_specs,
        out_specs=pl.BlockSpec((N_HEADS, t_new, KV_RANK), lambda b, j, pt: (0, b, 0)),
        scratch_shapes=[pltpu.VMEM((rows, 1), F32), pltpu.VMEM((rows, 1), F32),
                        pltpu.VMEM((rows, KV_RANK), F32)],
    )
    return pl.pallas_call(
        functools.partial(_decode_attn_kernel, n_pg),
        grid_spec=grid_spec,
        out_shape=jax.ShapeDtypeStruct((N_HEADS, n_tok, KV_RANK), F32),
        compiler_params=pltpu.CompilerParams(dimension_semantics=("parallel", "arbitrary"),
                                             vmem_limit_bytes=VMEM_LIMIT),
        name="decode_attend",
    )(pt_flat, q_lat, q_rope, ckv_new, kr_new, *([cache_ckv] * n_pg), *([cache_krope_t] * n_pg))


def _back_kernel(o_ref, u_ref, v_ref, x_ref, wuv_ref, wmix_ref, bmix_ref, ga_ref, gg_ref, wout_ref,
                 g2_ref, wgate_ref, wup_ref, wdown_ref, gf_ref, y_ref):
    tm = x_ref.shape[0]
    o_flat = jnp.concatenate([o_ref[hd].astype(BF16) for hd in range(N_HEADS)], axis=1)
    a = _dot(o_flat, wuv_ref[...])
    an = _rms(a, ga_ref[...])

    lane = lax.broadcasted_iota(jnp.int32, (CHUNK, GMLP_WIDTH), 1)
    wmix = wmix_ref[...]
    mixes = []
    for c in range(tm // CHUNK):
        vck = v_ref[c * CHUNK:(c + 1) * CHUNK, :].astype(BF16)
        bd = jnp.concatenate(
            [jnp.where((lane >= g * GROUP_DIM) & (lane < (g + 1) * GROUP_DIM), vck, jnp.zeros_like(vck))
             for g in range(N_GROUPS)], axis=0)
        mixes.append(_dot(wmix, bd) + bmix_ref[...])
    mix = mixes[0] if len(mixes) == 1 else jnp.concatenate(mixes, axis=0)
    gn = _rms(u_ref[...] * mix, gg_ref[...])

    merged = jnp.concatenate([an, gn], axis=1).astype(BF16)
    x1 = x_ref[...] + _dot(merged, wout_ref[...])
    h2 = _rms(x1, g2_ref[...]).astype(BF16)
    gate = _dot(h2, wgate_ref[...])
    up = _dot(h2, wup_ref[...])
    act = (gate * (1.0 / (1.0 + jnp.exp(-gate))) * up).astype(BF16)
    x2 = x1 + _dot(act, wdown_ref[...])
    y_ref[...] = _rms(x2, gf_ref[...])


def _back(o_lat, u, v, x, wts, *, tm, name):
    n_tok, d_model = x.shape
    row = lambda w: pl.BlockSpec((tm, w), lambda i: (i, 0))
    in_specs = [pl.BlockSpec((N_HEADS, tm, KV_RANK), lambda i: (0, i, 0)), row(GMLP_WIDTH), row(GMLP_WIDTH),
                row(d_model)]
    in_specs += [pl.BlockSpec(w.shape, lambda i, _nd=w.ndim: (0,) * _nd, pipeline_mode=pl.Buffered(1))
                 for w in wts]
    return pl.pallas_call(
        _back_kernel,
        grid=(n_tok // tm,),
        in_specs=in_specs,
        out_specs=row(d_model),
        out_shape=jax.ShapeDtypeStruct((n_tok, d_model), F32),
        compiler_params=pltpu.CompilerParams(dimension_semantics=("parallel",),
                                             vmem_limit_bytes=VMEM_LIMIT),
        name=name,
    )(o_lat, u, v, x, *wts)


def _rope_angles(pos):
    inv = ROPE_BASE ** (-jnp.arange(ROPE_HALF, dtype=F32) / ROPE_HALF)
    ang = pos.astype(F32)[:, None] * inv[None, :]
    return jnp.cos(ang), jnp.sin(ang)


def _lane_tiled(t, reps_rows):
    return jnp.tile(t, (reps_rows, LANES // ROPE_HALF))


def kernel(x_prompt, x_sample, cache_ckv, cache_krope, page_table, norm1_g, w_in, q_norm_g, w_q_up, kv_norm_g, w_uk, w_uv, v_norm_g, v_norm_b, w_spatial, b_spatial, out_norm_mla_g, out_norm_gmlp_g, w_out, norm2_g, w_gate, w_up, w_down, final_norm_g):
    batch, seq, d_model = x_prompt.shape
    dec_batch, t_new, _ = x_sample.shape
    depth = w_in.shape[0]
    past_len = page_table.shape[1] * PAGE_SIZE
    assert depth == 1 and seq % CHUNK == 0 and t_new <= CHUNK and CHUNK % t_new == 0

    xp = x_prompt.reshape(batch * seq, d_model)
    xs = x_sample.reshape(dec_batch * t_new, d_model)
    cos_p, sin_p = _rope_angles(jnp.arange(seq))
    cos_s, sin_s = _rope_angles(past_len + jnp.arange(t_new))
    tables_p = (_lane_tiled(cos_p, batch), _lane_tiled(sin_p, batch), cos_p.T, sin_p.T)
    tables_s = (_lane_tiled(cos_s, dec_batch), _lane_tiled(sin_s, dec_batch))

    l = 0
    row2 = lambda g: g.reshape(1, -1)
    wi = w_in[l]
    off_kr = Q_RANK + KV_RANK
    off_g = off_kr + QK_ROPE
    w_in_p = jnp.concatenate(
        [wi[:, :off_kr], wi[:, off_g:], jnp.zeros((d_model, PK_R1), wi.dtype), wi[:, off_kr:off_g],
         jnp.zeros((d_model, LANES - PK_END), wi.dtype)], axis=1).astype(BF16)
    wq = w_q_up[l].reshape(Q_RANK, N_HEADS, QK_NOPE + QK_ROPE)
    wq_p = jnp.concatenate([wq, jnp.zeros((Q_RANK, N_HEADS, LANES - PK_END), wq.dtype)], axis=2)
    wq_p = wq_p.reshape(Q_RANK, N_HEADS * LANES).astype(BF16)
    wuk = jnp.transpose(w_uk[l], (1, 2, 0))
    wuk_h = jnp.concatenate([wuk, jnp.zeros((N_HEADS, LANES - QK_NOPE, KV_RANK), wuk.dtype)],
                            axis=1).astype(BF16)
    wuk_cols = jnp.transpose(wuk_h, (2, 0, 1)).reshape(KV_RANK, N_HEADS * LANES)
    front_tail = (row2(v_norm_g[l]), row2(v_norm_b[l]))

    eye_h = jnp.eye(N_HEADS, dtype=w_uv.dtype)
    wuv_bd = jnp.einsum('rhv,hg->hrgv', w_uv[l], eye_h).reshape(N_HEADS * KV_RANK, MLA_WIDTH).astype(BF16)
    tril = jnp.tril(jnp.ones((CHUNK, CHUNK), dtype=bool))
    wmix_p = jnp.where(tril, w_spatial[l], 0)
    bmix_p = jnp.repeat(b_spatial[l].T, GROUP_DIM, axis=1)
    reps = CHUNK // t_new
    tril_s = jnp.tril(jnp.ones((t_new, t_new), dtype=bool))
    w_small = jnp.where(tril_s, w_spatial[l][:, :t_new, :t_new], 0)
    wmix_s = jnp.einsum('ab,gij->gaibj', jnp.eye(reps, dtype=w_small.dtype), w_small)
    wmix_s = wmix_s.reshape(N_GROUPS, CHUNK, CHUNK)
    bmix_s = jnp.tile(jnp.repeat(b_spatial[l][:, :t_new].T, GROUP_DIM, axis=1), (reps, 1))
    cat_groups = lambda w: jnp.transpose(w, (1, 0, 2)).reshape(CHUNK, N_GROUPS * CHUNK).astype(BF16)
    back_tail = (row2(out_norm_mla_g[l]), row2(out_norm_gmlp_g[l]), w_out[l].astype(BF16), row2(norm2_g[l]),
                 w_gate[l].astype(BF16), w_up[l].astype(BF16), w_down[l].astype(BF16), row2(final_norm_g))

    wts_p = (row2(norm1_g[l]), w_in_p, row2(q_norm_g[l]), wq_p.T, row2(kv_norm_g[l]), wuk_cols) + front_tail
    q_t, k_pk, ckv_p, ckv_t, kr_p, u_p, v_p = _front(xp, tables_p, wts_p, absorbed=False, tm=512, seq=seq)
    o_p = _prompt_attend(k_pk, q_t, ckv_t, batch=batch, seq=seq, blk=512)
    y_p = _back(o_p, u_p, v_p, xp, (wuv_bd, cat_groups(wmix_p), bmix_p) + back_tail, tm=256, name="back_prompt")

    wts_s = (row2(norm1_g[l]), w_in_p, row2(q_norm_g[l]), wq_p, row2(kv_norm_g[l]), wuk_h) + front_tail
    q_lat, q_rope, ckv_s, kr_s, u_s, v_s = _front(xs, tables_s, wts_s, absorbed=True, tm=256)
    o_s = _decode_attend(q_lat, q_rope, ckv_s, kr_s, cache_ckv, jnp.swapaxes(cache_krope, 2, 3), page_table,
                         t_new=t_new, n_pg=8)
    y_s = _back(o_s, u_s, v_s, xs, (wuv_bd, cat_groups(wmix_s), bmix_s) + back_tail, tm=256, name="back_decode")

    return (y_p.reshape(batch, seq, d_model),
            y_s.reshape(dec_batch, t_new, d_model),
            ckv_p.reshape(depth, batch, seq, KV_RANK),
            kr_p.reshape(depth, batch, seq, QK_ROPE),
            ckv_s.reshape(depth, dec_batch, t_new, KV_RANK),
            kr_s.reshape(depth, dec_batch, t_new, QK_ROPE),
            v_s.reshape(depth, dec_batch, t_new, GMLP_WIDTH))
```

```python
import functools
import math

import jax
import jax.numpy as jnp
from jax import lax
from jax.experimental import pallas as pl
from jax.experimental.pallas import tpu as pltpu

LANES = 128
MXU_DIM = 256
VMEM_LIMIT = 56 * 1024 * 1024

N_HEADS = 8
QK_NOPE = 64
QK_ROPE = 32
ROPE_HALF = QK_ROPE // 2
Q_RANK = 384
KV_RANK = 256
V_HEAD = 64
N_GROUPS = 8
GROUP_DIM = 64
GMLP_WIDTH = N_GROUPS * GROUP_DIM
MLA_WIDTH = N_HEADS * V_HEAD
CHUNK = 128
PAGE_SIZE = 128
ROPE_BASE = 10000.0
EPS = 1e-6
Q_SCALE = (QK_NOPE + QK_ROPE) ** -0.5 * math.log2(math.e)

COL_Q = 0
COL_KV = COL_Q + Q_RANK
COL_U = COL_KV + KV_RANK
COL_V = COL_U + GMLP_WIDTH
COL_KR = COL_V + GMLP_WIDTH
IN_COLS_PAD = COL_KR + LANES
PK_R1 = QK_NOPE
PK_R2 = QK_NOPE + ROPE_HALF
PK_END = QK_NOPE + QK_ROPE

BF16 = jnp.bfloat16
F32 = jnp.float32


def _rms(x, g):
    return x * lax.rsqrt(jnp.mean(x * x, axis=-1, keepdims=True) + EPS) * g


def _dot(a, b):
    return jnp.dot(a, b, preferred_element_type=F32)


def _dot_nt(a, b):
    return lax.dot_general(a, b, (((1,), (1,)), ((), ())), preferred_element_type=F32)


def _rope_packed(t, c_mul, s_up, s_down):
    return (t * c_mul
            + pltpu.roll(t, LANES - ROPE_HALF, axis=1) * s_up
            + pltpu.roll(t, ROPE_HALF, axis=1) * s_down)


def _front_kernel(absorbed, *refs):
    if absorbed:
        (x_ref, cos_ref, sin_ref, g1_ref, w_in_ref, gq_ref, wq_ref, gkv_ref, wuk_ref, vg_ref, vb_ref,
         qlat_ref, qrope_ref, ckv_ref, kr_ref, u_ref, v_ref) = refs
    else:
        (x_ref, cos_ref, sin_ref, cost_ref, sint_ref, g1_ref, w_in_ref, gq_ref, wq_ref, gkv_ref, wuk_ref,
         vg_ref, vb_ref, qt_ref, kpk_ref, ckv_ref, ckvt_ref, kr_ref, u_ref, v_ref) = refs
    x = x_ref[...]
    h = _rms(x, g1_ref[...]).astype(BF16)
    z = _dot(h, w_in_ref[...])

    cos_t = cos_ref[...]
    sin_t = sin_ref[...]
    lane = lax.broadcasted_iota(jnp.int32, cos_t.shape, 1)
    in_r1 = (lane >= PK_R1) & (lane < PK_R2)
    in_r2 = (lane >= PK_R2) & (lane < PK_END)

    ckv = _rms(z[:, COL_KV:COL_U], gkv_ref[...])
    ckv_ref[...] = ckv
    k_c = jnp.where(in_r1 | in_r2, cos_t, 0.0)
    k_up = jnp.where(in_r1, -sin_t, 0.0)
    k_dn = jnp.where(in_r2, sin_t, 0.0)
    kr = _rope_packed(z[:, COL_KR:COL_KR + LANES], k_c, k_up, k_dn)
    kr_ref[...] = kr[:, PK_R1:PK_END]

    cqn = _rms(z[:, COL_Q:COL_KV], gq_ref[...]).astype(BF16)
    if absorbed:
        qp = _dot(cqn, wq_ref[...])
        q_c = jnp.where(lane < PK_R1, Q_SCALE, jnp.where(lane < PK_END, Q_SCALE * cos_t, 0.0))
        q_up = jnp.where(in_r1, -Q_SCALE * sin_t, 0.0)
        q_dn = jnp.where(in_r2, Q_SCALE * sin_t, 0.0)
        for hd in range(N_HEADS):
            qh = _rope_packed(qp[:, hd * LANES:(hd + 1) * LANES], q_c, q_up, q_dn)
            qlat_ref[hd] = _dot(qh.astype(BF16), wuk_ref[hd])
            qrope_ref[hd] = qh[:, PK_R1:PK_END]
    else:
        qpt = _dot_nt(wq_ref[...], cqn)
        c_t = cost_ref[...] * Q_SCALE
        s_t = sint_ref[...] * Q_SCALE
        zpad = jnp.zeros((LANES - PK_END, qpt.shape[1]), F32)
        for hd in range(N_HEADS):
            blk = qpt[hd * LANES:(hd + 1) * LANES]
            x1 = blk[PK_R1:PK_R2]
            x2 = blk[PK_R2:PK_END]
            qt = jnp.concatenate([blk[:PK_R1] * Q_SCALE, x1 * c_t - x2 * s_t, x1 * s_t + x2 * c_t, zpad],
                                 axis=0)
            qt_ref[hd] = qt.astype(BF16)
        ckv_b = ckv.astype(BF16)
        knope = _dot(ckv_b, wuk_ref[...])
        for hd in range(N_HEADS):
            kpk_ref[hd] = (knope[:, hd * LANES:(hd + 1) * LANES] + kr).astype(BF16)
        ckvt_ref[...] = jnp.transpose(ckv).astype(BF16)

    zg = z[:, COL_U:COL_KR]
    zg = 0.5 * zg * (1.0 + lax.erf(zg * math.sqrt(0.5)))
    u_ref[...] = zg[:, :GMLP_WIDTH]
    vv = zg[:, GMLP_WIDTH:]
    mu = jnp.mean(vv, axis=-1, keepdims=True)
    vc = vv - mu
    v_ref[...] = vc * lax.rsqrt(jnp.mean(vc * vc, axis=-1, keepdims=True) + EPS) * vg_ref[...] + vb_ref[...]


def _const_spec(shape):
    nd = len(shape)
    return pl.BlockSpec(shape, lambda i, _nd=nd: (0,) * _nd)


def _front(x, tables, wts, *, absorbed, tm, seq=None):
    n_tok, d_model = x.shape
    grid = (n_tok // tm,)
    row = lambda w: pl.BlockSpec((tm, w), lambda i: (i, 0))
    head = lambda w, dt: (jax.ShapeDtypeStruct((N_HEADS, n_tok, w), dt),
                          pl.BlockSpec((N_HEADS, tm, w), lambda i: (0, i, 0)))
    flat = lambda w, dt: (jax.ShapeDtypeStruct((n_tok, w), dt), row(w))
    table_specs = [row(LANES), row(LANES)]
    if absorbed:
        outs = [head(KV_RANK, F32), head(QK_ROPE, F32), flat(KV_RANK, F32), flat(QK_ROPE, F32),
                flat(GMLP_WIDTH, F32), flat(GMLP_WIDTH, F32)]
    else:
        spb = seq // tm
        nb = n_tok // seq
        table_specs += [pl.BlockSpec((ROPE_HALF, tm), lambda i: (0, i % spb))] * 2
        qt = (jax.ShapeDtypeStruct((nb, N_HEADS, LANES, seq), BF16),
              pl.BlockSpec((None, N_HEADS, LANES, tm), lambda i: (i // spb, 0, 0, i % spb)))
        ckvt = (jax.ShapeDtypeStruct((nb, KV_RANK, seq), BF16),
                pl.BlockSpec((None, KV_RANK, tm), lambda i: (i // spb, 0, i % spb)))
        outs = [qt, head(LANES, BF16), flat(KV_RANK, F32), ckvt, flat(QK_ROPE, F32),
                flat(GMLP_WIDTH, F32), flat(GMLP_WIDTH, F32)]
    ins = [x] + list(tables) + list(wts)
    in_specs = [row(d_model)] + table_specs + [_const_spec(w.shape) for w in wts]
    return pl.pallas_call(
        functools.partial(_front_kernel, absorbed),
        grid=grid,
        in_specs=in_specs,
        out_specs=[o[1] for o in outs],
        out_shape=[o[0] for o in outs],
        compiler_params=pltpu.CompilerParams(dimension_semantics=("parallel",),
                                             vmem_limit_bytes=VMEM_LIMIT),
        name="front_absorbed" if absorbed else "front_prompt",
    )(*ins)


def _prompt_attn_kernel(qi_ref, ki_ref, k_ref, qt_ref, vt_ref, o_ref, m_ref, l_ref, acc_ref):
    p_id = pl.program_id(1)
    qi = qi_ref[p_id]
    ki = ki_ref[p_id]
    blk_k = k_ref.shape[1]
    blk_q = qt_ref.shape[2]
    qw = MXU_DIM

    @pl.when(ki == 0)
    def _():
        m_ref[...] = jnp.full(m_ref.shape, -jnp.inf, F32)
        l_ref[...] = jnp.zeros(l_ref.shape, F32)
        acc_ref[...] = jnp.zeros(acc_ref.shape, F32)

    def step(diagonal):
        for c in range(blk_q // qw):
            cols = slice(c * qw, (c + 1) * qw)
            nk = min(blk_k, (c + 1) * qw) if diagonal else blk_k
            vt = vt_ref[:, :nk]
            if diagonal:
                kpos = lax.broadcasted_iota(jnp.int32, (nk, qw), 0)
                qpos = lax.broadcasted_iota(jnp.int32, (nk, qw), 1) + c * qw
                keep = kpos <= qpos
            for hd in range(N_HEADS):
                st = _dot(k_ref[hd, :nk, :], qt_ref[hd, :, cols])
                if diagonal:
                    st = jnp.where(keep, st, -jnp.inf)
                m_prev = m_ref[hd, :, cols]
                m_new = jnp.maximum(m_prev, jnp.max(st, axis=0, keepdims=True))
                alpha = jnp.exp2(m_prev - m_new)
                pt = jnp.exp2(st - m_new)
                l_ref[hd, :, cols] = alpha * l_ref[hd, :, cols] + jnp.sum(pt, axis=0, keepdims=True)
                acc_ref[hd, :, cols] = alpha * acc_ref[hd, :, cols] + _dot(vt, pt.astype(BF16))
                m_ref[hd, :, cols] = m_new

    @pl.when(ki < qi)
    def _():
        step(False)

    @pl.when(ki == qi)
    def _():
        step(True)
        for hd in range(N_HEADS):
            ot = acc_ref[hd] * (1.0 / l_ref[hd])
            o_ref[hd] = jnp.transpose(ot).astype(o_ref.dtype)


def _prompt_attend(k_pk, q_t, ckv_t, *, batch, seq, blk):
    nq = seq // blk
    pairs = [(q, k) for q in range(nq) for k in range(q + 1)]
    qi = jnp.asarray([p[0] for p in pairs], jnp.int32)
    ki = jnp.asarray([p[1] for p in pairs], jnp.int32)
    n_tok = batch * seq
    grid_spec = pltpu.PrefetchScalarGridSpec(
        num_scalar_prefetch=2,
        grid=(batch, len(pairs)),
        in_specs=[
            pl.BlockSpec((N_HEADS, blk, LANES), lambda b, p, qi, ki: (0, b * nq + ki[p], 0)),
            pl.BlockSpec((None, N_HEADS, LANES, blk), lambda b, p, qi, ki: (b, 0, 0, qi[p])),
            pl.BlockSpec((None, KV_RANK, blk), lambda b, p, qi, ki: (b, 0, ki[p])),
        ],
        out_specs=pl.BlockSpec((N_HEADS, blk, KV_RANK), lambda b, p, qi, ki: (0, b * nq + qi[p], 0)),
        scratch_shapes=[pltpu.VMEM((N_HEADS, 1, blk), F32), pltpu.VMEM((N_HEADS, 1, blk), F32),
                        pltpu.VMEM((N_HEADS, KV_RANK, blk), F32)],
    )
    return pl.pallas_call(
        _prompt_attn_kernel,
        grid_spec=grid_spec,
        out_shape=jax.ShapeDtypeStruct((N_HEADS, n_tok, KV_RANK), BF16),
        compiler_params=pltpu.CompilerParams(dimension_semantics=("parallel", "arbitrary"),
                                             vmem_limit_bytes=VMEM_LIMIT),
        name="prompt_attend",
    )(qi, ki, k_pk, q_t, ckv_t)


N_SLOTS = 2
DECODE_SPLIT = 2


def _local_softmax(s, v_b):
    m = jnp.max(s, axis=-1, keepdims=True)
    p = jnp.exp2(s - m)
    return m, jnp.sum(p, axis=-1, keepdims=True), _dot(p.astype(BF16), v_b)


def _merge_softmax(parts, m_ref, l_ref, acc_ref):
    m_run = m_ref[...]
    m_new = m_run
    for m, _, _ in parts:
        m_new = jnp.maximum(m_new, m)
    w_run = jnp.exp2(m_run - m_new)
    l_new = w_run * l_ref[...]
    acc_new = w_run * acc_ref[...]
    for m, l, acc in parts:
        w = jnp.exp2(m - m_new)
        l_new = l_new + w * l
        acc_new = acc_new + w * acc
    m_ref[...] = m_new
    l_ref[...] = l_new
    acc_ref[...] = acc_new


def _decode_attn_kernel(n_pg, pt_ref, ql_ref, qr_ref, cn_ref, rn_ref, ckv_hbm, krt_hbm, o_ref,
                        kbuf, rbuf, sems, m_ref, l_ref, acc_ref):
    b = pl.program_id(0)
    j = pl.program_id(1)
    nj = pl.num_programs(1)
    chunk = b * nj + j
    n_chunks = pl.num_programs(0) * nj
    slot = chunk % N_SLOTS

    def page_copies(ck, sl, i):
        page = pt_ref[ck * n_pg + i]
        rows = pl.ds(i * PAGE_SIZE, PAGE_SIZE)
        return (pltpu.make_async_copy(ckv_hbm.at[0, page], kbuf.at[sl, rows], sems.at[0, sl]),
                pltpu.make_async_copy(krt_hbm.at[0, page], rbuf.at[sl, :, rows], sems.at[1, sl]))

    def start_chunk(ck, sl):
        for i in range(n_pg):
            for cp in page_copies(ck, sl, i):
                cp.start()

    @pl.when(chunk == 0)
    def _():
        start_chunk(chunk, slot)

    @pl.when(chunk + 1 < n_chunks)
    def _():
        start_chunk(chunk + 1, (chunk + 1) % N_SLOTS)

    @pl.when(j == 0)
    def _():
        m_ref[...] = jnp.full(m_ref.shape, -jnp.inf, F32)
        l_ref[...] = jnp.zeros(l_ref.shape, F32)
        acc_ref[...] = jnp.zeros(acc_ref.shape, F32)

    n_rows = N_HEADS * ql_ref.shape[1]
    ql = ql_ref[...].reshape(n_rows, KV_RANK).astype(BF16)
    qr = qr_ref[...].reshape(n_rows, QK_ROPE).astype(BF16)

    for i in range(n_pg):
        for cp in page_copies(chunk, slot, i):
            cp.wait()

    keys = n_pg * PAGE_SIZE // DECODE_SPLIT
    parts = []
    for c in range(DECODE_SPLIT):
        kb = kbuf[slot, c * keys:(c + 1) * keys, :].astype(BF16)
        rbt = rbuf[slot, :, c * keys:(c + 1) * keys].astype(BF16)
        parts.append(_local_softmax(_dot_nt(ql, kb) + _dot(qr, rbt), kb))
    _merge_softmax(parts, m_ref, l_ref, acc_ref)

    @pl.when(j == nj - 1)
    def _():
        t_new = cn_ref.shape[0]
        cb = cn_ref[...].astype(BF16)
        s_new = _dot_nt(ql, cb) + _dot_nt(qr, rn_ref[...].astype(BF16))
        r_pos = lax.broadcasted_iota(jnp.int32, s_new.shape, 0) % t_new
        c_pos = lax.broadcasted_iota(jnp.int32, s_new.shape, 1)
        s_new = jnp.where(c_pos <= r_pos, s_new, -jnp.inf)
        _merge_softmax([_local_softmax(s_new, cb)], m_ref, l_ref, acc_ref)
        o = acc_ref[...] * (1.0 / l_ref[...])
        o_ref[...] = o.reshape(o_ref.shape)


def _decode_attend(q_lat, q_rope, ckv_new, kr_new, cache_ckv, cache_krope_t, page_table, *, t_new, n_pg):
    dec_batch, n_pages = page_table.shape
    n_tok = dec_batch * t_new
    assert n_pages % n_pg == 0
    pt_flat = page_table.reshape(-1)
    in_specs = [
        pl.BlockSpec((N_HEADS, t_new, KV_RANK), lambda b, j, pt: (0, b, 0)),
        pl.BlockSpec((N_HEADS, t_new, QK_ROPE), lambda b, j, pt: (0, b, 0)),
        pl.BlockSpec((t_new, KV_RANK), lambda b, j, pt: (b, 0)),
        pl.BlockSpec((t_new, QK_ROPE), lambda b, j, pt: (b, 0)),
        pl.BlockSpec(memory_space=pl.ANY),
        pl.BlockSpec(memory_space=pl.ANY),
    ]
    rows = N_HEADS * t_new
    grid_spec = pltpu.PrefetchScalarGridSpec(
        num_scalar_prefetch=1,
        grid=(dec_batch, n_pages // n_pg),
        in_specs=in_specs,
        out_specs=pl.BlockSpec((N_HEADS, t_new, KV_RANK), lambda b, j, pt: (0, b, 0)),
        scratch_shapes=[pltpu.VMEM((N_SLOTS, n_pg * PAGE_SIZE, KV_RANK), F32),
                        pltpu.VMEM((N_SLOTS, QK_ROPE, n_pg * PAGE_SIZE), F32),
                        pltpu.SemaphoreType.DMA((2, N_SLOTS)),
                        pltpu.VMEM((rows, 1), F32), pltpu.VMEM((rows, 1), F32),
                        pltpu.VMEM((rows, KV_RANK), F32)],
    )
    return pl.pallas_call(
        functools.partial(_decode_attn_kernel, n_pg),
        grid_spec=grid_spec,
        out_shape=jax.ShapeDtypeStruct((N_HEADS, n_tok, KV_RANK), F32),
        compiler_params=pltpu.CompilerParams(dimension_semantics=("arbitrary", "arbitrary"),
                                             vmem_limit_bytes=VMEM_LIMIT),
        name="decode_attend",
    )(pt_flat, q_lat, q_rope, ckv_new, kr_new, cache_ckv, cache_krope_t)


def _back_kernel(o_ref, u_ref, v_ref, x_ref, wuv_ref, wmix_ref, bmix_ref, ga_ref, gg_ref, wout_ref,
                 g2_ref, wgate_ref, wup_ref, wdown_ref, gf_ref, y_ref):
    tm = x_ref.shape[0]
    o_flat = jnp.concatenate([o_ref[hd].astype(BF16) for hd in range(N_HEADS)], axis=1)
    a = _dot(o_flat, wuv_ref[...])
    an = _rms(a, ga_ref[...])

    lane = lax.broadcasted_iota(jnp.int32, (CHUNK, GMLP_WIDTH), 1)
    wmix = wmix_ref[...]
    mixes = []
    for c in range(tm // CHUNK):
        vck = v_ref[c * CHUNK:(c + 1) * CHUNK, :].astype(BF16)
        bd = jnp.concatenate(
            [jnp.where((lane >= g * GROUP_DIM) & (lane < (g + 1) * GROUP_DIM), vck, jnp.zeros_like(vck))
             for g in range(N_GROUPS)], axis=0)
        mixes.append(_dot(wmix, bd) + bmix_ref[...])
    mix = mixes[0] if len(mixes) == 1 else jnp.concatenate(mixes, axis=0)
    gn = _rms(u_ref[...] * mix, gg_ref[...])

    merged = jnp.concatenate([an, gn], axis=1).astype(BF16)
    x1 = x_ref[...] + _dot(merged, wout_ref[...])
    h2 = _rms(x1, g2_ref[...]).astype(BF16)
    gate = _dot(h2, wgate_ref[...])
    up = _dot(h2, wup_ref[...])
    act = (gate * (1.0 / (1.0 + jnp.exp(-gate))) * up).astype(BF16)
    x2 = x1 + _dot(act, wdown_ref[...])
    y_ref[...] = _rms(x2, gf_ref[...])


def _back(o_lat, u, v, x, wts, *, tm, name):
    n_tok, d_model = x.shape
    row = lambda w: pl.BlockSpec((tm, w), lambda i: (i, 0))
    in_specs = [pl.BlockSpec((N_HEADS, tm, KV_RANK), lambda i: (0, i, 0)), row(GMLP_WIDTH), row(GMLP_WIDTH),
                row(d_model)]
    in_specs += [pl.BlockSpec(w.shape, lambda i, _nd=w.ndim: (0,) * _nd, pipeline_mode=pl.Buffered(1))
                 for w in wts]
    return pl.pallas_call(
        _back_kernel,
        grid=(n_tok // tm,),
        in_specs=in_specs,
        out_specs=row(d_model),
        out_shape=jax.ShapeDtypeStruct((n_tok, d_model), F32),
        compiler_params=pltpu.CompilerParams(dimension_semantics=("parallel",),
                                             vmem_limit_bytes=VMEM_LIMIT),
        name=name,
    )(o_lat, u, v, x, *wts)


def _rope_angles(pos):
    inv = ROPE_BASE ** (-jnp.arange(ROPE_HALF, dtype=F32) / ROPE_HALF)
    ang = pos.astype(F32)[:, None] * inv[None, :]
    return jnp.cos(ang), jnp.sin(ang)


def _lane_tiled(t, reps_rows):
    return jnp.tile(t, (reps_rows, LANES // ROPE_HALF))


def kernel(x_prompt, x_sample, cache_ckv, cache_krope, page_table, norm1_g, w_in, q_norm_g, w_q_up, kv_norm_g, w_uk, w_uv, v_norm_g, v_norm_b, w_spatial, b_spatial, out_norm_mla_g, out_norm_gmlp_g, w_out, norm2_g, w_gate, w_up, w_down, final_norm_g):
    batch, seq, d_model = x_prompt.shape
    dec_batch, t_new, _ = x_sample.shape
    depth = w_in.shape[0]
    past_len = page_table.shape[1] * PAGE_SIZE
    assert depth == 1 and seq % CHUNK == 0 and t_new <= CHUNK and CHUNK % t_new == 0

    xp = x_prompt.reshape(batch * seq, d_model)
    xs = x_sample.reshape(dec_batch * t_new, d_model)
    cos_p, sin_p = _rope_angles(jnp.arange(seq))
    cos_s, sin_s = _rope_angles(past_len + jnp.arange(t_new))
    tables_p = (_lane_tiled(cos_p, batch), _lane_tiled(sin_p, batch), cos_p.T, sin_p.T)
    tables_s = (_lane_tiled(cos_s, dec_batch), _lane_tiled(sin_s, dec_batch))

    l = 0
    row2 = lambda g: g.reshape(1, -1)
    wi = w_in[l]
    off_kr = Q_RANK + KV_RANK
    off_g = off_kr + QK_ROPE
    w_in_p = jnp.concatenate(
        [wi[:, :off_kr], wi[:, off_g:], jnp.zeros((d_model, PK_R1), wi.dtype), wi[:, off_kr:off_g],
         jnp.zeros((d_model, LANES - PK_END), wi.dtype)], axis=1).astype(BF16)
    wq = w_q_up[l].reshape(Q_RANK, N_HEADS, QK_NOPE + QK_ROPE)
    wq_p = jnp.concatenate([wq, jnp.zeros((Q_RANK, N_HEADS, LANES - PK_END), wq.dtype)], axis=2)
    wq_p = wq_p.reshape(Q_RANK, N_HEADS * LANES).astype(BF16)
    wuk = jnp.transpose(w_uk[l], (1, 2, 0))
    wuk_h = jnp.concatenate([wuk, jnp.zeros((N_HEADS, LANES - QK_NOPE, KV_RANK), wuk.dtype)],
                            axis=1).astype(BF16)
    wuk_cols = jnp.transpose(wuk_h, (2, 0, 1)).reshape(KV_RANK, N_HEADS * LANES)
    front_tail = (row2(v_norm_g[l]), row2(v_norm_b[l]))

    eye_h = jnp.eye(N_HEADS, dtype=w_uv.dtype)
    wuv_bd = jnp.einsum('rhv,hg->hrgv', w_uv[l], eye_h).reshape(N_HEADS * KV_RANK, MLA_WIDTH).astype(BF16)
    tril = jnp.tril(jnp.ones((CHUNK, CHUNK), dtype=bool))
    wmix_p = jnp.where(tril, w_spatial[l], 0)
    bmix_p = jnp.repeat(b_spatial[l].T, GROUP_DIM, axis=1)
    reps = CHUNK // t_new
    tril_s = jnp.tril(jnp.ones((t_new, t_new), dtype=bool))
    w_small = jnp.where(tril_s, w_spatial[l][:, :t_new, :t_new], 0)
    wmix_s = jnp.einsum('ab,gij->gaibj', jnp.eye(reps, dtype=w_small.dtype), w_small)
    wmix_s = wmix_s.reshape(N_GROUPS, CHUNK, CHUNK)
    bmix_s = jnp.tile(jnp.repeat(b_spatial[l][:, :t_new].T, GROUP_DIM, axis=1), (reps, 1))
    cat_groups = lambda w: jnp.transpose(w, (1, 0, 2)).reshape(CHUNK, N_GROUPS * CHUNK).astype(BF16)
    back_tail = (row2(out_norm_mla_g[l]), row2(out_norm_gmlp_g[l]), w_out[l].astype(BF16), row2(norm2_g[l]),
                 w_gate[l].astype(BF16), w_up[l].astype(BF16), w_down[l].astype(BF16), row2(final_norm_g))

    wts_p = (row2(norm1_g[l]), w_in_p, row2(q_norm_g[l]), wq_p.T, row2(kv_norm_g[l]), wuk_cols) + front_tail
    q_t, k_pk, ckv_p, ckv_t, kr_p, u_p, v_p = _front(xp, tables_p, wts_p, absorbed=False, tm=512, seq=seq)
    o_p = _prompt_attend(k_pk, q_t, ckv_t, batch=batch, seq=seq, blk=512)
    y_p = _back(o_p, u_p, v_p, xp, (wuv_bd, cat_groups(wmix_p), bmix_p) + back_tail, tm=256, name="back_prompt")

    wts_s = (row2(norm1_g[l]), w_in_p, row2(q_norm_g[l]), wq_p, row2(kv_norm_g[l]), wuk_h) + front_tail
    q_lat, q_rope, ckv_s, kr_s, u_s, v_s = _front(xs, tables_s, wts_s, absorbed=True, tm=256)
    o_s = _decode_attend(q_lat, q_rope, ckv_s, kr_s, cache_ckv, jnp.swapaxes(cache_krope, 2, 3), page_table,
                         t_new=t_new, n_pg=16)
    y_s = _back(o_s, u_s, v_s, xs, (wuv_bd, cat_groups(wmix_s), bmix_s) + back_tail, tm=256, name="back_decode")

    return (y_p.reshape(batch, seq, d_model),
            y_s.reshape(dec_batch, t_new, d_model),
            ckv_p.reshape(depth, batch, seq, KV_RANK),
            kr_p.reshape(depth, batch, seq, QK_ROPE),
            ckv_s.reshape(depth, dec_batch, t_new, KV_RANK),
            kr_s.reshape(depth, dec_batch, t_new, QK_ROPE),
            v_s.reshape(depth, dec_batch, t_new, GMLP_WIDTH))
```

```python
import functools
import math

import jax
import jax.numpy as jnp
from jax import lax
from jax.experimental import pallas as pl
from jax.experimental.pallas import tpu as pltpu

LANES = 128
SUBLANES = 8
MXU_DIM = 256
VMEM_LIMIT = 56 * 1024 * 1024

N_HEADS = 8
QK_NOPE = 64
QK_ROPE = 32
ROPE_HALF = QK_ROPE // 2
Q_RANK = 384
KV_RANK = 256
V_HEAD = 64
N_GROUPS = 8
GROUP_DIM = 64
GMLP_WIDTH = N_GROUPS * GROUP_DIM
MLA_WIDTH = N_HEADS * V_HEAD
CHUNK = 128
PAGE_SIZE = 128
ROPE_BASE = 10000.0
EPS = 1e-6
Q_SCALE = (QK_NOPE + QK_ROPE) ** -0.5 * math.log2(math.e)

COL_Q = 0
COL_KV = COL_Q + Q_RANK
COL_U = COL_KV + KV_RANK
COL_V = COL_U + GMLP_WIDTH
COL_KR = COL_V + GMLP_WIDTH
IN_COLS_PAD = COL_KR + LANES
PK_R1 = QK_NOPE
PK_R2 = QK_NOPE + ROPE_HALF
PK_END = QK_NOPE + QK_ROPE

BF16 = jnp.bfloat16
F32 = jnp.float32


def _rms(x, g):
    return x * lax.rsqrt(jnp.mean(x * x, axis=-1, keepdims=True) + EPS) * g


def _dot(a, b):
    return jnp.dot(a, b, preferred_element_type=F32)


def _dot_nt(a, b):
    return lax.dot_general(a, b, (((1,), (1,)), ((), ())), preferred_element_type=F32)


def _rope_packed(t, c_mul, s_up, s_down):
    return (t * c_mul
            + pltpu.roll(t, LANES - ROPE_HALF, axis=1) * s_up
            + pltpu.roll(t, ROPE_HALF, axis=1) * s_down)


def _front_kernel(absorbed, *refs):
    if absorbed:
        (x_ref, cos_ref, sin_ref, g1_ref, w_in_ref, gq_ref, wq_ref, gkv_ref, wuk_ref, vg_ref, vb_ref,
         qlat_ref, qrope_ref, ckv_ref, kr_ref, u_ref, v_ref) = refs
    else:
        (x_ref, cos_ref, sin_ref, cost_ref, sint_ref, g1_ref, w_in_ref, gq_ref, wq_ref, gkv_ref, wuk_ref,
         wuvt_ref, vg_ref, vb_ref, qt_ref, kpk_ref, ckv_ref, vt_ref, kr_ref, u_ref, v_ref) = refs
    x = x_ref[...]
    h = _rms(x, g1_ref[...]).astype(BF16)
    z = _dot(h, w_in_ref[...])

    cos_t = cos_ref[...]
    sin_t = sin_ref[...]
    lane = lax.broadcasted_iota(jnp.int32, cos_t.shape, 1)
    in_r1 = (lane >= PK_R1) & (lane < PK_R2)
    in_r2 = (lane >= PK_R2) & (lane < PK_END)

    ckv = _rms(z[:, COL_KV:COL_U], gkv_ref[...])
    ckv_ref[...] = ckv
    k_c = jnp.where(in_r1 | in_r2, cos_t, 0.0)
    k_up = jnp.where(in_r1, -sin_t, 0.0)
    k_dn = jnp.where(in_r2, sin_t, 0.0)
    kr = _rope_packed(z[:, COL_KR:COL_KR + LANES], k_c, k_up, k_dn)
    kr_ref[...] = kr[:, PK_R1:PK_END]

    cqn = _rms(z[:, COL_Q:COL_KV], gq_ref[...]).astype(BF16)
    if absorbed:
        qp = _dot(cqn, wq_ref[...])
        q_c = jnp.where(lane < PK_R1, Q_SCALE, jnp.where(lane < PK_END, Q_SCALE * cos_t, 0.0))
        q_up = jnp.where(in_r1, -Q_SCALE * sin_t, 0.0)
        q_dn = jnp.where(in_r2, Q_SCALE * sin_t, 0.0)
        for hd in range(N_HEADS):
            qh = _rope_packed(qp[:, hd * LANES:(hd + 1) * LANES], q_c, q_up, q_dn)
            qlat_ref[hd] = _dot(qh.astype(BF16), wuk_ref[hd])
            qrope_ref[hd] = qh[:, PK_R1:PK_END]
    else:
        qpt = _dot_nt(wq_ref[...], cqn)
        c_t = cost_ref[...] * Q_SCALE
        s_t = sint_ref[...] * Q_SCALE
        zpad = jnp.zeros((LANES - PK_END, qpt.shape[1]), F32)
        for hd in range(N_HEADS):
            blk = qpt[hd * LANES:(hd + 1) * LANES]
            x1 = blk[PK_R1:PK_R2]
            x2 = blk[PK_R2:PK_END]
            qt = jnp.concatenate([blk[:PK_R1] * Q_SCALE, x1 * c_t - x2 * s_t, x1 * s_t + x2 * c_t, zpad],
                                 axis=0)
            qt_ref[hd] = qt.astype(BF16)
        ckv_b = ckv.astype(BF16)
        knope = _dot(ckv_b, wuk_ref[...])
        for hd in range(N_HEADS):
            kpk_ref[hd] = (knope[:, hd * LANES:(hd + 1) * LANES] + kr).astype(BF16)
        vt_ref[...] = _dot_nt(wuvt_ref[...], ckv_b).astype(BF16)

    zg = z[:, COL_U:COL_KR]
    zg = 0.5 * zg * (1.0 + lax.erf(zg * math.sqrt(0.5)))
    u_ref[...] = zg[:, :GMLP_WIDTH]
    vv = zg[:, GMLP_WIDTH:]
    mu = jnp.mean(vv, axis=-1, keepdims=True)
    vc = vv - mu
    v_ref[...] = vc * lax.rsqrt(jnp.mean(vc * vc, axis=-1, keepdims=True) + EPS) * vg_ref[...] + vb_ref[...]


def _const_spec(shape):
    nd = len(shape)
    return pl.BlockSpec(shape, lambda i, _nd=nd: (0,) * _nd)


def _front(x, tables, wts, *, absorbed, tm, seq=None):
    n_tok, d_model = x.shape
    grid = (n_tok // tm,)
    row = lambda w: pl.BlockSpec((tm, w), lambda i: (i, 0))
    head = lambda w, dt: (jax.ShapeDtypeStruct((N_HEADS, n_tok, w), dt),
                          pl.BlockSpec((N_HEADS, tm, w), lambda i: (0, i, 0)))
    flat = lambda w, dt: (jax.ShapeDtypeStruct((n_tok, w), dt), row(w))
    table_specs = [row(LANES), row(LANES)]
    if absorbed:
        outs = [head(KV_RANK, F32), head(QK_ROPE, F32), flat(KV_RANK, F32), flat(QK_ROPE, F32),
                flat(GMLP_WIDTH, F32), flat(GMLP_WIDTH, F32)]
    else:
        spb = seq // tm
        nb = n_tok // seq
        table_specs += [pl.BlockSpec((ROPE_HALF, tm), lambda i: (0, i % spb))] * 2
        qt = (jax.ShapeDtypeStruct((nb, N_HEADS, LANES, seq), BF16),
              pl.BlockSpec((None, N_HEADS, LANES, tm), lambda i: (i // spb, 0, 0, i % spb)))
        vt = (jax.ShapeDtypeStruct((nb, MLA_WIDTH, seq), BF16),
              pl.BlockSpec((None, MLA_WIDTH, tm), lambda i: (i // spb, 0, i % spb)))
        outs = [qt, head(LANES, BF16), flat(KV_RANK, F32), vt, flat(QK_ROPE, F32),
                flat(GMLP_WIDTH, F32), flat(GMLP_WIDTH, F32)]
    ins = [x] + list(tables) + list(wts)
    in_specs = [row(d_model)] + table_specs + [_const_spec(w.shape) for w in wts]
    return pl.pallas_call(
        functools.partial(_front_kernel, absorbed),
        grid=grid,
        in_specs=in_specs,
        out_specs=[o[1] for o in outs],
        out_shape=[o[0] for o in outs],
        compiler_params=pltpu.CompilerParams(dimension_semantics=("parallel",),
                                             vmem_limit_bytes=VMEM_LIMIT),
        name="front_absorbed" if absorbed else "front_prompt",
    )(*ins)


ATTN_UNIT_COLS = 2 * MXU_DIM
SOFTMAX_ROW_CHUNKS = 4
def _prompt_attn_kernel(qi_ref, ki_ref, k_ref, qt_ref, vt_ref, a_ref, m_ref, l_ref, acc_ref):
    p_id = pl.program_id(1)
    qi = qi_ref[p_id]
    ki = ki_ref[p_id]
    blk_k = k_ref.shape[1]
    blk_q = qt_ref.shape[2]
    qw = ATTN_UNIT_COLS

    @pl.when(ki == 0)
    def _():
        m_ref[...] = jnp.full(m_ref.shape, -jnp.inf, F32)
        l_ref[...] = jnp.zeros(l_ref.shape, F32)
        acc_ref[...] = jnp.zeros(acc_ref.shape, F32)

    def step(diagonal):
        n_keys = lambda c: min(blk_k, (c + 1) * qw) if diagonal else blk_k
        units = [(c, hd) for c in range(blk_q // qw) for hd in range(N_HEADS)]

        def scores(c, hd):
            nk = n_keys(c)
            st = _dot(k_ref[hd, :nk, :], qt_ref[hd, :, c * qw:(c + 1) * qw])
            if diagonal:
                kpos = lax.broadcasted_iota(jnp.int32, (nk, qw), 0)
                qpos = lax.broadcasted_iota(jnp.int32, (nk, qw), 1) + c * qw
                st = jnp.where(kpos <= qpos, st, -jnp.inf)
            return st

        def rows(st, r):
            rk = st.shape[0] // SOFTMAX_ROW_CHUNKS
            return st[r * rk:(r + 1) * rk]

        def fold8(x, op):
            return op(x.reshape(x.shape[0] // SUBLANES, SUBLANES, x.shape[1]), axis=0)

        n_units = len(units)
        st_cur = scores(*units[0])
        mloc_cur = jnp.max(st_cur, axis=0, keepdims=True)
        st_nxt = scores(*units[1])
        for idx, (c, hd) in enumerate(units):
            cols = slice(c * qw, (c + 1) * qw)
            m_prev = m_ref[hd, :, cols]
            m_new = jnp.maximum(m_prev, mloc_cur)
            alpha = jnp.exp2(m_prev - m_new)
            st_nn = scores(*units[idx + 2]) if idx + 2 < n_units else None
            mx8 = None
            sum8 = None
            pts = []
            for r in range(SOFTMAX_ROW_CHUNKS):
                if idx + 1 < n_units:
                    part = fold8(rows(st_nxt, r), jnp.max)
                    mx8 = part if mx8 is None else jnp.maximum(mx8, part)
                p_r = jnp.exp2(rows(st_cur, r) - m_new)
                psum = fold8(p_r, jnp.sum)
                sum8 = psum if sum8 is None else sum8 + psum
                pts.append(p_r.astype(BF16))
            pt = jnp.concatenate(pts, axis=0)
            mloc_nxt = None if mx8 is None else jnp.max(mx8, axis=0, keepdims=True)
            l_ref[hd, :, cols] = alpha * l_ref[hd, :, cols] + jnp.sum(sum8, axis=0, keepdims=True)
            vt = vt_ref[hd * V_HEAD:(hd + 1) * V_HEAD, :n_keys(c)]
            acc_ref[hd, :, cols] = alpha * acc_ref[hd, :, cols] + _dot(vt, pt)
            m_ref[hd, :, cols] = m_new
            st_cur, mloc_cur, st_nxt = st_nxt, mloc_nxt, st_nn

    @pl.when(ki < qi)
    def _():
        step(False)

    @pl.when(ki == qi)
    def _():
        step(True)
        at = jnp.concatenate([acc_ref[hd] * (1.0 / l_ref[hd]) for hd in range(N_HEADS)], axis=0)
        a_ref[...] = jnp.transpose(at)


def _prompt_attend(k_pk, q_t, v_t, *, batch, seq, blk):
    nq = seq // blk
    pairs = [(q, k) for q in range(nq) for k in range(q + 1)]
    qi = jnp.asarray([p[0] for p in pairs], jnp.int32)
    ki = jnp.asarray([p[1] for p in pairs], jnp.int32)
    n_tok = batch * seq
    grid_spec = pltpu.PrefetchScalarGridSpec(
        num_scalar_prefetch=2,
        grid=(batch, len(pairs)),
        in_specs=[
            pl.BlockSpec((N_HEADS, blk, LANES), lambda b, p, qi, ki: (0, b * nq + ki[p], 0)),
            pl.BlockSpec((None, N_HEADS, LANES, blk), lambda b, p, qi, ki: (b, 0, 0, qi[p])),
            pl.BlockSpec((None, MLA_WIDTH, blk), lambda b, p, qi, ki: (b, 0, ki[p])),
        ],
        out_specs=pl.BlockSpec((blk, MLA_WIDTH), lambda b, p, qi, ki: (b * nq + qi[p], 0)),
        scratch_shapes=[pltpu.VMEM((N_HEADS, 1, blk), F32), pltpu.VMEM((N_HEADS, 1, blk), F32),
                        pltpu.VMEM((N_HEADS, V_HEAD, blk), F32)],
    )
    return pl.pallas_call(
        _prompt_attn_kernel,
        grid_spec=grid_spec,
        out_shape=jax.ShapeDtypeStruct((n_tok, MLA_WIDTH), F32),
        compiler_params=pltpu.CompilerParams(dimension_semantics=("parallel", "arbitrary"),
                                             vmem_limit_bytes=VMEM_LIMIT),
        name="prompt_attend",
    )(qi, ki, k_pk, q_t, v_t)


N_SLOTS = 3
DECODE_SPLIT = 2


def _local_softmax(s, v_b):
    m = jnp.max(s, axis=-1, keepdims=True)
    p = jnp.exp2(s - m)
    return m, jnp.sum(p, axis=-1, keepdims=True), _dot(p.astype(BF16), v_b)


def _merge_softmax(parts, m_ref, l_ref, acc_ref):
    m_run = m_ref[...]
    m_new = m_run
    for m, _, _ in parts:
        m_new = jnp.maximum(m_new, m)
    w_run = jnp.exp2(m_run - m_new)
    l_new = w_run * l_ref[...]
    acc_new = w_run * acc_ref[...]
    for m, l, acc in parts:
        w = jnp.exp2(m - m_new)
        l_new = l_new + w * l
        acc_new = acc_new + w * acc
    m_ref[...] = m_new
    l_ref[...] = l_new
    acc_ref[...] = acc_new


def _decode_attn_kernel(n_pg, pt_ref, ql_ref, qr_ref, cn_ref, rn_ref, ckv_hbm, krt_hbm, o_ref,
                        kbuf, rbuf, sems, m_ref, l_ref, acc_ref):
    b = pl.program_id(0)
    j = pl.program_id(1)
    nj = pl.num_programs(1)
    chunk = b * nj + j
    n_chunks = pl.num_programs(0) * nj
    slot = chunk % N_SLOTS

    def page_copies(ck, sl, i):
        page = pt_ref[ck * n_pg + i]
        rows = pl.ds(i * PAGE_SIZE, PAGE_SIZE)
        return (pltpu.make_async_copy(ckv_hbm.at[0, page], kbuf.at[sl, rows], sems.at[0, sl]),
                pltpu.make_async_copy(krt_hbm.at[0, page], rbuf.at[sl, :, rows], sems.at[1, sl]))

    def start_chunk(ck, sl):
        for i in range(n_pg):
            for cp in page_copies(ck, sl, i):
                cp.start()

    ahead = N_SLOTS - 1

    @pl.when(chunk == 0)
    def _():
        for ck in range(ahead):
            start_chunk(ck, ck)

    @pl.when(chunk + ahead < n_chunks)
    def _():
        start_chunk(chunk + ahead, (chunk + ahead) % N_SLOTS)

    @pl.when(j == 0)
    def _():
        m_ref[...] = jnp.full(m_ref.shape, -jnp.inf, F32)
        l_ref[...] = jnp.zeros(l_ref.shape, F32)
        acc_ref[...] = jnp.zeros(acc_ref.shape, F32)

    n_rows = N_HEADS * ql_ref.shape[1]
    ql = ql_ref[...].reshape(n_rows, KV_RANK).astype(BF16)
    qr = qr_ref[...].reshape(n_rows, QK_ROPE).astype(BF16)

    for i in range(n_pg):
        for cp in page_copies(chunk, slot, i):
            cp.wait()

    keys = n_pg * PAGE_SIZE // DECODE_SPLIT
    parts = []
    for c in range(DECODE_SPLIT):
        kb = kbuf[slot, c * keys:(c + 1) * keys, :].astype(BF16)
        rbt = rbuf[slot, :, c * keys:(c + 1) * keys].astype(BF16)
        parts.append(_local_softmax(_dot_nt(ql, kb) + _dot(qr, rbt), kb))
    _merge_softmax(parts, m_ref, l_ref, acc_ref)

    @pl.when(j == nj - 1)
    def _():
        t_new = cn_ref.shape[0]
        cb = cn_ref[...].astype(BF16)
        s_new = _dot_nt(ql, cb) + _dot_nt(qr, rn_ref[...].astype(BF16))
        r_pos = lax.broadcasted_iota(jnp.int32, s_new.shape, 0) % t_new
        c_pos = lax.broadcasted_iota(jnp.int32, s_new.shape, 1)
        s_new = jnp.where(c_pos <= r_pos, s_new, -jnp.inf)
        _merge_softmax([_local_softmax(s_new, cb)], m_ref, l_ref, acc_ref)
        o = acc_ref[...] * (1.0 / l_ref[...])
        o_ref[...] = o.reshape(o_ref.shape)


def _decode_attend(q_lat, q_rope, ckv_new, kr_new, cache_ckv, cache_krope_t, page_table, *, t_new, n_pg):
    dec_batch, n_pages = page_table.shape
    n_tok = dec_batch * t_new
    assert n_pages % n_pg == 0 and dec_batch * (n_pages // n_pg) >= N_SLOTS
    pt_flat = page_table.reshape(-1)
    in_specs = [
        pl.BlockSpec((N_HEADS, t_new, KV_RANK), lambda b, j, pt: (0, b, 0)),
        pl.BlockSpec((N_HEADS, t_new, QK_ROPE), lambda b, j, pt: (0, b, 0)),
        pl.BlockSpec((t_new, KV_RANK), lambda b, j, pt: (b, 0)),
        pl.BlockSpec((t_new, QK_ROPE), lambda b, j, pt: (b, 0)),
        pl.BlockSpec(memory_space=pl.ANY),
        pl.BlockSpec(memory_space=pl.ANY),
    ]
    rows = N_HEADS * t_new
    grid_spec = pltpu.PrefetchScalarGridSpec(
        num_scalar_prefetch=1,
        grid=(dec_batch, n_pages // n_pg),
        in_specs=in_specs,
        out_specs=pl.BlockSpec((N_HEADS, t_new, KV_RANK), lambda b, j, pt: (0, b, 0)),
        scratch_shapes=[pltpu.VMEM((N_SLOTS, n_pg * PAGE_SIZE, KV_RANK), F32),
                        pltpu.VMEM((N_SLOTS, QK_ROPE, n_pg * PAGE_SIZE), F32),
                        pltpu.SemaphoreType.DMA((2, N_SLOTS)),
                        pltpu.VMEM((rows, 1), F32), pltpu.VMEM((rows, 1), F32),
                        pltpu.VMEM((rows, KV_RANK), F32)],
    )
    return pl.pallas_call(
        functools.partial(_decode_attn_kernel, n_pg),
        grid_spec=grid_spec,
        out_shape=jax.ShapeDtypeStruct((N_HEADS, n_tok, KV_RANK), F32),
        compiler_params=pltpu.CompilerParams(dimension_semantics=("arbitrary", "arbitrary"),
                                             vmem_limit_bytes=VMEM_LIMIT),
        name="decode_attend",
    )(pt_flat, q_lat, q_rope, ckv_new, kr_new, cache_ckv, cache_krope_t)


def _back_kernel(latent_attn, *refs):
    if latent_attn:
        (o_ref, u_ref, v_ref, x_ref, wuv_ref, wmix_ref, bmix_ref, ga_ref, gg_ref, wout_ref,
         g2_ref, wgate_ref, wup_ref, wdown_ref, gf_ref, y_ref) = refs
        o_flat = jnp.concatenate([o_ref[hd].astype(BF16) for hd in range(N_HEADS)], axis=1)
        a = _dot(o_flat, wuv_ref[...])
    else:
        (a_ref, u_ref, v_ref, x_ref, wmix_ref, bmix_ref, ga_ref, gg_ref, wout_ref,
         g2_ref, wgate_ref, wup_ref, wdown_ref, gf_ref, y_ref) = refs
        a = a_ref[...]
    tm = x_ref.shape[0]
    an = _rms(a, ga_ref[...])

    lane = lax.broadcasted_iota(jnp.int32, (CHUNK, GMLP_WIDTH), 1)
    wmix = wmix_ref[...]
    mixes = []
    for c in range(tm // CHUNK):
        vck = v_ref[c * CHUNK:(c + 1) * CHUNK, :].astype(BF16)
        bd = jnp.concatenate(
            [jnp.where((lane >= g * GROUP_DIM) & (lane < (g + 1) * GROUP_DIM), vck, jnp.zeros_like(vck))
             for g in range(N_GROUPS)], axis=0)
        mixes.append(_dot(wmix, bd) + bmix_ref[...])
    mix = mixes[0] if len(mixes) == 1 else jnp.concatenate(mixes, axis=0)
    gn = _rms(u_ref[...] * mix, gg_ref[...])

    merged = jnp.concatenate([an, gn], axis=1).astype(BF16)
    x1 = x_ref[...] + _dot(merged, wout_ref[...])
    h2 = _rms(x1, g2_ref[...]).astype(BF16)
    gate = _dot(h2, wgate_ref[...])
    up = _dot(h2, wup_ref[...])
    act = (gate * (1.0 / (1.0 + jnp.exp(-gate))) * up).astype(BF16)
    x2 = x1 + _dot(act, wdown_ref[...])
    y_ref[...] = _rms(x2, gf_ref[...])


def _back(attn, u, v, x, wts, *, tm, name):
    n_tok, d_model = x.shape
    latent_attn = attn.ndim == 3
    row = lambda w: pl.BlockSpec((tm, w), lambda i: (i, 0))
    attn_spec = pl.BlockSpec((N_HEADS, tm, KV_RANK), lambda i: (0, i, 0)) if latent_attn else row(MLA_WIDTH)
    in_specs = [attn_spec, row(GMLP_WIDTH), row(GMLP_WIDTH), row(d_model)]
    in_specs += [pl.BlockSpec(w.shape, lambda i, _nd=w.ndim: (0,) * _nd, pipeline_mode=pl.Buffered(1))
                 for w in wts]
    return pl.pallas_call(
        functools.partial(_back_kernel, latent_attn),
        grid=(n_tok // tm,),
        in_specs=in_specs,
        out_specs=row(d_model),
        out_shape=jax.ShapeDtypeStruct((n_tok, d_model), F32),
        compiler_params=pltpu.CompilerParams(dimension_semantics=("parallel",),
                                             vmem_limit_bytes=VMEM_LIMIT),
        name=name,
    )(attn, u, v, x, *wts)


def _rope_angles(pos):
    inv = ROPE_BASE ** (-jnp.arange(ROPE_HALF, dtype=F32) / ROPE_HALF)
    ang = pos.astype(F32)[:, None] * inv[None, :]
    return jnp.cos(ang), jnp.sin(ang)


def _lane_tiled(t, reps_rows):
    return jnp.tile(t, (reps_rows, LANES // ROPE_HALF))


def kernel(x_prompt, x_sample, cache_ckv, cache_krope, page_table, norm1_g, w_in, q_norm_g, w_q_up, kv_norm_g, w_uk, w_uv, v_norm_g, v_norm_b, w_spatial, b_spatial, out_norm_mla_g, out_norm_gmlp_g, w_out, norm2_g, w_gate, w_up, w_down, final_norm_g):
    batch, seq, d_model = x_prompt.shape
    dec_batch, t_new, _ = x_sample.shape
    depth = w_in.shape[0]
    past_len = page_table.shape[1] * PAGE_SIZE
    assert depth == 1 and seq % CHUNK == 0 and t_new <= CHUNK and CHUNK % t_new == 0

    xp = x_prompt.reshape(batch * seq, d_model)
    xs = x_sample.reshape(dec_batch * t_new, d_model)
    cos_p, sin_p = _rope_angles(jnp.arange(seq))
    cos_s, sin_s = _rope_angles(past_len + jnp.arange(t_new))
    tables_p = (_lane_tiled(cos_p, batch), _lane_tiled(sin_p, batch), cos_p.T, sin_p.T)
    tables_s = (_lane_tiled(cos_s, dec_batch), _lane_tiled(sin_s, dec_batch))

    l = 0
    row2 = lambda g: g.reshape(1, -1)
    wi = w_in[l]
    off_kr = Q_RANK + KV_RANK
    off_g = off_kr + QK_ROPE
    w_in_p = jnp.concatenate(
        [wi[:, :off_kr], wi[:, off_g:], jnp.zeros((d_model, PK_R1), wi.dtype), wi[:, off_kr:off_g],
         jnp.zeros((d_model, LANES - PK_END), wi.dtype)], axis=1).astype(BF16)
    wq = w_q_up[l].reshape(Q_RANK, N_HEADS, QK_NOPE + QK_ROPE)
    wq_p = jnp.concatenate([wq, jnp.zeros((Q_RANK, N_HEADS, LANES - PK_END), wq.dtype)], axis=2)
    wq_p = wq_p.reshape(Q_RANK, N_HEADS * LANES).astype(BF16)
    wuk = jnp.transpose(w_uk[l], (1, 2, 0))
    wuk_h = jnp.concatenate([wuk, jnp.zeros((N_HEADS, LANES - QK_NOPE, KV_RANK), wuk.dtype)],
                            axis=1).astype(BF16)
    wuk_cols = jnp.transpose(wuk_h, (2, 0, 1)).reshape(KV_RANK, N_HEADS * LANES)
    front_tail = (row2(v_norm_g[l]), row2(v_norm_b[l]))

    eye_h = jnp.eye(N_HEADS, dtype=w_uv.dtype)
    wuv_bd = jnp.einsum('rhv,hg->hrgv', w_uv[l], eye_h).reshape(N_HEADS * KV_RANK, MLA_WIDTH).astype(BF16)
    tril = jnp.tril(jnp.ones((CHUNK, CHUNK), dtype=bool))
    wmix_p = jnp.where(tril, w_spatial[l], 0)
    bmix_p = jnp.repeat(b_spatial[l].T, GROUP_DIM, axis=1)
    reps = CHUNK // t_new
    tril_s = jnp.tril(jnp.ones((t_new, t_new), dtype=bool))
    w_small = jnp.where(tril_s, w_spatial[l][:, :t_new, :t_new], 0)
    wmix_s = jnp.einsum('ab,gij->gaibj', jnp.eye(reps, dtype=w_small.dtype), w_small)
    wmix_s = wmix_s.reshape(N_GROUPS, CHUNK, CHUNK)
    bmix_s = jnp.tile(jnp.repeat(b_spatial[l][:, :t_new].T, GROUP_DIM, axis=1), (reps, 1))
    cat_groups = lambda w: jnp.transpose(w, (1, 0, 2)).reshape(CHUNK, N_GROUPS * CHUNK).astype(BF16)
    back_tail = (row2(out_norm_mla_g[l]), row2(out_norm_gmlp_g[l]), w_out[l].astype(BF16), row2(norm2_g[l]),
                 w_gate[l].astype(BF16), w_up[l].astype(BF16), w_down[l].astype(BF16), row2(final_norm_g))

    wuv_t = w_uv[l].reshape(KV_RANK, MLA_WIDTH).T.astype(BF16)
    wts_p = (row2(norm1_g[l]), w_in_p, row2(q_norm_g[l]), wq_p.T, row2(kv_norm_g[l]), wuk_cols, wuv_t) + front_tail
    q_t, k_pk, ckv_p, v_t, kr_p, u_p, v_p = _front(xp, tables_p, wts_p, absorbed=False, tm=512, seq=seq)
    a_p = _prompt_attend(k_pk, q_t, v_t, batch=batch, seq=seq, blk=512)
    y_p = _back(a_p, u_p, v_p, xp, (cat_groups(wmix_p), bmix_p) + back_tail, tm=256, name="back_prompt")

    wts_s = (row2(norm1_g[l]), w_in_p, row2(q_norm_g[l]), wq_p, row2(kv_norm_g[l]), wuk_h) + front_tail
    q_lat, q_rope, ckv_s, kr_s, u_s, v_s = _front(xs, tables_s, wts_s, absorbed=True, tm=256)
    o_s = _decode_attend(q_lat, q_rope, ckv_s, kr_s, cache_ckv, jnp.swapaxes(cache_krope, 2, 3), page_table,
                         t_new=t_new, n_pg=16)
    y_s = _back(o_s, u_s, v_s, xs, (wuv_bd, cat_groups(wmix_s), bmix_s) + back_tail, tm=256, name="back_decode")

    return (y_p.reshape(batch, seq, d_model),
            y_s.reshape(dec_batch, t_new, d_model),
            ckv_p.reshape(depth, batch, seq, KV_RANK),
            kr_p.reshape(depth, batch, seq, QK_ROPE),
            ckv_s.reshape(depth, dec_batch, t_new, KV_RANK),
            kr_s.reshape(depth, dec_batch, t_new, QK_ROPE),
            v_s.reshape(depth, dec_batch, t_new, GMLP_WIDTH))
```

```python
import functools
import math

import jax
import jax.numpy as jnp
from jax import lax
from jax.experimental import pallas as pl
from jax.experimental.pallas import tpu as pltpu

LANES = 128
SUBLANES = 8
MXU_DIM = 256
VMEM_LIMIT = 56 * 1024 * 1024

N_HEADS = 8
QK_NOPE = 64
QK_ROPE = 32
ROPE_HALF = QK_ROPE // 2
Q_RANK = 384
KV_RANK = 256
V_HEAD = 64
N_GROUPS = 8
GROUP_DIM = 64
GMLP_WIDTH = N_GROUPS * GROUP_DIM
MLA_WIDTH = N_HEADS * V_HEAD
CHUNK = 128
PAGE_SIZE = 128
ROPE_BASE = 10000.0
EPS = 1e-6
Q_SCALE = (QK_NOPE + QK_ROPE) ** -0.5 * math.log2(math.e)

COL_Q = 0
COL_KV = COL_Q + Q_RANK
COL_U = COL_KV + KV_RANK
COL_V = COL_U + GMLP_WIDTH
COL_KR = COL_V + GMLP_WIDTH
IN_COLS_PAD = COL_KR + LANES
PK_R1 = QK_NOPE
PK_R2 = QK_NOPE + ROPE_HALF
PK_END = QK_NOPE + QK_ROPE

BF16 = jnp.bfloat16
F32 = jnp.float32


def _rms(x, g):
    return x * lax.rsqrt(jnp.mean(x * x, axis=-1, keepdims=True) + EPS) * g


def _dot(a, b):
    return jnp.dot(a, b, preferred_element_type=F32)


def _dot_nt(a, b):
    return lax.dot_general(a, b, (((1,), (1,)), ((), ())), preferred_element_type=F32)


def _rope_packed(t, c_mul, s_up, s_down):
    return (t * c_mul
            + pltpu.roll(t, LANES - ROPE_HALF, axis=1) * s_up
            + pltpu.roll(t, ROPE_HALF, axis=1) * s_down)


def _front_kernel(absorbed, *refs):
    if absorbed:
        (x_ref, cos_ref, sin_ref, g1_ref, w_in_ref, gq_ref, wq_ref, gkv_ref, wuk_ref, vg_ref, vb_ref,
         qlat_ref, qrope_ref, ckv_ref, kr_ref, u_ref, v_ref) = refs
    else:
        (x_ref, cos_ref, sin_ref, cost_ref, sint_ref, g1_ref, w_in_ref, gq_ref, wq_ref, gkv_ref, wuk_ref,
         wuvt_ref, vg_ref, vb_ref, qt_ref, kpk_ref, ckv_ref, vt_ref, kr_ref, u_ref, v_ref) = refs
    x = x_ref[...]
    h = _rms(x, g1_ref[...]).astype(BF16)
    z = _dot(h, w_in_ref[...])

    cos_t = cos_ref[...]
    sin_t = sin_ref[...]
    lane = lax.broadcasted_iota(jnp.int32, cos_t.shape, 1)
    in_r1 = (lane >= PK_R1) & (lane < PK_R2)
    in_r2 = (lane >= PK_R2) & (lane < PK_END)

    ckv = _rms(z[:, COL_KV:COL_U], gkv_ref[...])
    ckv_ref[...] = ckv
    k_c = jnp.where(in_r1 | in_r2, cos_t, 0.0)
    k_up = jnp.where(in_r1, -sin_t, 0.0)
    k_dn = jnp.where(in_r2, sin_t, 0.0)
    kr = _rope_packed(z[:, COL_KR:COL_KR + LANES], k_c, k_up, k_dn)
    kr_ref[...] = kr[:, PK_R1:PK_END]

    cqn = _rms(z[:, COL_Q:COL_KV], gq_ref[...]).astype(BF16)
    if absorbed:
        qp = _dot(cqn, wq_ref[...])
        q_c = jnp.where(lane < PK_R1, Q_SCALE, jnp.where(lane < PK_END, Q_SCALE * cos_t, 0.0))
        q_up = jnp.where(in_r1, -Q_SCALE * sin_t, 0.0)
        q_dn = jnp.where(in_r2, Q_SCALE * sin_t, 0.0)
        for hd in range(N_HEADS):
            qh = _rope_packed(qp[:, hd * LANES:(hd + 1) * LANES], q_c, q_up, q_dn)
            qlat_ref[hd] = _dot(qh.astype(BF16), wuk_ref[hd])
            qrope_ref[hd] = qh[:, PK_R1:PK_END]
    else:
        qpt = _dot_nt(wq_ref[...], cqn)
        c_t = cost_ref[...] * Q_SCALE
        s_t = sint_ref[...] * Q_SCALE
        zpad = jnp.zeros((LANES - PK_END, qpt.shape[1]), F32)
        for hd in range(N_HEADS):
            blk = qpt[hd * LANES:(hd + 1) * LANES]
            x1 = blk[PK_R1:PK_R2]
            x2 = blk[PK_R2:PK_END]
            qt = jnp.concatenate([blk[:PK_R1] * Q_SCALE, x1 * c_t - x2 * s_t, x1 * s_t + x2 * c_t, zpad],
                                 axis=0)
            qt_ref[hd] = qt.astype(BF16)
        ckv_b = ckv.astype(BF16)
        knope = _dot(ckv_b, wuk_ref[...])
        for hd in range(N_HEADS):
            kpk_ref[hd] = (knope[:, hd * LANES:(hd + 1) * LANES] + kr).astype(BF16)
        vt_ref[...] = _dot_nt(wuvt_ref[...], ckv_b).astype(BF16)

    zg = z[:, COL_U:COL_KR]
    zg = 0.5 * zg * (1.0 + lax.erf(zg * math.sqrt(0.5)))
    u_ref[...] = zg[:, :GMLP_WIDTH]
    vv = zg[:, GMLP_WIDTH:]
    mu = jnp.mean(vv, axis=-1, keepdims=True)
    vc = vv - mu
    v_ref[...] = vc * lax.rsqrt(jnp.mean(vc * vc, axis=-1, keepdims=True) + EPS) * vg_ref[...] + vb_ref[...]


def _const_spec(shape):
    nd = len(shape)
    return pl.BlockSpec(shape, lambda i, _nd=nd: (0,) * _nd)


def _front(x, tables, wts, *, absorbed, tm, seq=None):
    n_tok, d_model = x.shape
    grid = (n_tok // tm,)
    row = lambda w: pl.BlockSpec((tm, w), lambda i: (i, 0))
    head = lambda w, dt: (jax.ShapeDtypeStruct((N_HEADS, n_tok, w), dt),
                          pl.BlockSpec((N_HEADS, tm, w), lambda i: (0, i, 0)))
    flat = lambda w, dt: (jax.ShapeDtypeStruct((n_tok, w), dt), row(w))
    tab_blocks = tables[0].shape[0] // tm
    table_specs = [pl.BlockSpec((tm, LANES), lambda i: (i % tab_blocks, 0))] * 2
    if absorbed:
        outs = [head(KV_RANK, F32), head(QK_ROPE, F32), flat(KV_RANK, F32), flat(QK_ROPE, F32),
                flat(GMLP_WIDTH, F32), flat(GMLP_WIDTH, F32)]
    else:
        spb = seq // tm
        nb = n_tok // seq
        table_specs += [pl.BlockSpec((ROPE_HALF, tm), lambda i: (0, i % spb))] * 2
        qt = (jax.ShapeDtypeStruct((nb, N_HEADS, LANES, seq), BF16),
              pl.BlockSpec((None, N_HEADS, LANES, tm), lambda i: (i // spb, 0, 0, i % spb)))
        vt = (jax.ShapeDtypeStruct((nb, MLA_WIDTH, seq), BF16),
              pl.BlockSpec((None, MLA_WIDTH, tm), lambda i: (i // spb, 0, i % spb)))
        outs = [qt, head(LANES, BF16), flat(KV_RANK, F32), vt, flat(QK_ROPE, F32),
                flat(GMLP_WIDTH, F32), flat(GMLP_WIDTH, F32)]
    ins = [x] + list(tables) + list(wts)
    in_specs = [row(d_model)] + table_specs + [_const_spec(w.shape) for w in wts]
    return pl.pallas_call(
        functools.partial(_front_kernel, absorbed),
        grid=grid,
        in_specs=in_specs,
        out_specs=[o[1] for o in outs],
        out_shape=[o[0] for o in outs],
        compiler_params=pltpu.CompilerParams(dimension_semantics=("parallel",),
                                             vmem_limit_bytes=VMEM_LIMIT),
        name="front_absorbed" if absorbed else "front_prompt",
    )(*ins)


ATTN_UNIT_COLS = 2 * MXU_DIM
SOFTMAX_ROW_CHUNKS = 4
def _prompt_attn_kernel(qi_ref, ki_ref, k_ref, qt_ref, vt_ref, a_ref, m_ref, l_ref, acc_ref):
    p_id = pl.program_id(1)
    qi = qi_ref[p_id]
    ki = ki_ref[p_id]
    blk_k = k_ref.shape[1]
    blk_q = qt_ref.shape[2]
    qw = ATTN_UNIT_COLS

    @pl.when(ki == 0)
    def _():
        m_ref[...] = jnp.full(m_ref.shape, -jnp.inf, F32)
        l_ref[...] = jnp.zeros(l_ref.shape, F32)
        acc_ref[...] = jnp.zeros(acc_ref.shape, F32)

    def step(diagonal):
        n_keys = lambda c: min(blk_k, (c + 1) * qw) if diagonal else blk_k
        units = [(c, hd) for c in range(blk_q // qw) for hd in range(N_HEADS)]

        def scores(c, hd):
            nk = n_keys(c)
            st = _dot(k_ref[hd, :nk, :], qt_ref[hd, :, c * qw:(c + 1) * qw])
            if diagonal:
                kpos = lax.broadcasted_iota(jnp.int32, (nk, qw), 0)
                qpos = lax.broadcasted_iota(jnp.int32, (nk, qw), 1) + c * qw
                st = jnp.where(kpos <= qpos, st, -jnp.inf)
            return st

        def rows(st, r):
            rk = st.shape[0] // SOFTMAX_ROW_CHUNKS
            return st[r * rk:(r + 1) * rk]

        def fold8(x, op):
            return op(x.reshape(x.shape[0] // SUBLANES, SUBLANES, x.shape[1]), axis=0)

        n_units = len(units)
        st_cur = scores(*units[0])
        mloc_cur = jnp.max(st_cur, axis=0, keepdims=True)
        st_nxt = scores(*units[1])
        for idx, (c, hd) in enumerate(units):
            cols = slice(c * qw, (c + 1) * qw)
            m_prev = m_ref[hd, :, cols]
            m_new = jnp.maximum(m_prev, mloc_cur)
            alpha = jnp.exp2(m_prev - m_new)
            st_nn = scores(*units[idx + 2]) if idx + 2 < n_units else None
            mx8 = None
            sum8 = None
            pts = []
            for r in range(SOFTMAX_ROW_CHUNKS):
                if idx + 1 < n_units:
                    part = fold8(rows(st_nxt, r), jnp.max)
                    mx8 = part if mx8 is None else jnp.maximum(mx8, part)
                p_r = jnp.exp2(rows(st_cur, r) - m_new)
                psum = fold8(p_r, jnp.sum)
                sum8 = psum if sum8 is None else sum8 + psum
                pts.append(p_r.astype(BF16))
            pt = jnp.concatenate(pts, axis=0)
            mloc_nxt = None if mx8 is None else jnp.max(mx8, axis=0, keepdims=True)
            l_ref[hd, :, cols] = alpha * l_ref[hd, :, cols] + jnp.sum(sum8, axis=0, keepdims=True)
            vt = vt_ref[hd * V_HEAD:(hd + 1) * V_HEAD, :n_keys(c)]
            acc_ref[hd, :, cols] = alpha * acc_ref[hd, :, cols] + _dot(vt, pt)
            m_ref[hd, :, cols] = m_new
            st_cur, mloc_cur, st_nxt = st_nxt, mloc_nxt, st_nn

    @pl.when(ki < qi)
    def _():
        step(False)

    @pl.when(ki == qi)
    def _():
        step(True)
        at = jnp.concatenate([acc_ref[hd] * (1.0 / l_ref[hd]) for hd in range(N_HEADS)], axis=0)
        a_ref[...] = jnp.transpose(at)


def _prompt_attend(k_pk, q_t, v_t, *, batch, seq, blk):
    nq = seq // blk
    pairs = [(q, k) for q in range(nq) for k in range(q + 1)]
    qi = jnp.asarray([p[0] for p in pairs], jnp.int32)
    ki = jnp.asarray([p[1] for p in pairs], jnp.int32)
    n_tok = batch * seq
    grid_spec = pltpu.PrefetchScalarGridSpec(
        num_scalar_prefetch=2,
        grid=(batch, len(pairs)),
        in_specs=[
            pl.BlockSpec((N_HEADS, blk, LANES), lambda b, p, qi, ki: (0, b * nq + ki[p], 0)),
            pl.BlockSpec((None, N_HEADS, LANES, blk), lambda b, p, qi, ki: (b, 0, 0, qi[p])),
            pl.BlockSpec((None, MLA_WIDTH, blk), lambda b, p, qi, ki: (b, 0, ki[p])),
        ],
        out_specs=pl.BlockSpec((blk, MLA_WIDTH), lambda b, p, qi, ki: (b * nq + qi[p], 0)),
        scratch_shapes=[pltpu.VMEM((N_HEADS, 1, blk), F32), pltpu.VMEM((N_HEADS, 1, blk), F32),
                        pltpu.VMEM((N_HEADS, V_HEAD, blk), F32)],
    )
    return pl.pallas_call(
        _prompt_attn_kernel,
        grid_spec=grid_spec,
        out_shape=jax.ShapeDtypeStruct((n_tok, MLA_WIDTH), F32),
        compiler_params=pltpu.CompilerParams(dimension_semantics=("parallel", "arbitrary"),
                                             vmem_limit_bytes=VMEM_LIMIT),
        name="prompt_attend",
    )(qi, ki, k_pk, q_t, v_t)


N_SLOTS = 3
DECODE_SPLIT = 4


def _local_softmax(s, v_b):
    m = jnp.max(s, axis=-1, keepdims=True)
    p = jnp.exp2(s - m)
    return m, jnp.sum(p, axis=-1, keepdims=True), _dot(p.astype(BF16), v_b)


def _merge_softmax(parts, m_ref, l_ref, acc_ref):
    m_run = m_ref[...]
    m_new = m_run
    for m, _, _ in parts:
        m_new = jnp.maximum(m_new, m)
    w_run = jnp.exp2(m_run - m_new)
    l_new = w_run * l_ref[...]
    acc_new = w_run * acc_ref[...]
    for m, l, acc in parts:
        w = jnp.exp2(m - m_new)
        l_new = l_new + w * l
        acc_new = acc_new + w * acc
    m_ref[...] = m_new
    l_ref[...] = l_new
    acc_ref[...] = acc_new


def _decode_attn_kernel(n_pg, pt_ref, ql_ref, qr_ref, cn_ref, rn_ref, ckv_hbm, krt_hbm, o_ref,
                        kbuf, rbuf, sems, m_ref, l_ref, acc_ref):
    b = pl.program_id(0)
    j = pl.program_id(1)
    nj = pl.num_programs(1)
    chunk = b * nj + j
    n_chunks = pl.num_programs(0) * nj
    slot = chunk % N_SLOTS

    def page_copies(ck, sl, i):
        page = pt_ref[ck * n_pg + i]
        rows = pl.ds(i * PAGE_SIZE, PAGE_SIZE)
        return (pltpu.make_async_copy(ckv_hbm.at[0, page], kbuf.at[sl, rows], sems.at[0, sl]),
                pltpu.make_async_copy(krt_hbm.at[0, page], rbuf.at[sl, :, rows], sems.at[1, sl]))

    def start_chunk(ck, sl):
        for i in range(n_pg):
            for cp in page_copies(ck, sl, i):
                cp.start()

    ahead = N_SLOTS - 1

    @pl.when(chunk == 0)
    def _():
        for ck in range(ahead):
            start_chunk(ck, ck)

    @pl.when(chunk + ahead < n_chunks)
    def _():
        start_chunk(chunk + ahead, (chunk + ahead) % N_SLOTS)

    @pl.when(j == 0)
    def _():
        m_ref[...] = jnp.full(m_ref.shape, -jnp.inf, F32)
        l_ref[...] = jnp.zeros(l_ref.shape, F32)
        acc_ref[...] = jnp.zeros(acc_ref.shape, F32)

    n_rows = N_HEADS * ql_ref.shape[1]
    ql = ql_ref[...].reshape(n_rows, KV_RANK).astype(BF16)
    qr = qr_ref[...].reshape(n_rows, QK_ROPE).astype(BF16)

    for i in range(n_pg):
        for cp in page_copies(chunk, slot, i):
            cp.wait()

    keys = n_pg * PAGE_SIZE // DECODE_SPLIT
    kbs, scores = [], []
    for c in range(DECODE_SPLIT):
        kb = kbuf[slot, c * keys:(c + 1) * keys, :].astype(BF16)
        rbt = rbuf[slot, :, c * keys:(c + 1) * keys].astype(BF16)
        kbs.append(kb)
        scores.append(_dot_nt(ql, kb) + _dot(qr, rbt))
    parts = [_local_softmax(s, kb) for s, kb in zip(scores, kbs)]
    _merge_softmax(parts, m_ref, l_ref, acc_ref)

    @pl.when(j == nj - 1)
    def _():
        t_new = cn_ref.shape[0]
        cb = cn_ref[...].astype(BF16)
        s_new = _dot_nt(ql, cb) + _dot_nt(qr, rn_ref[...].astype(BF16))
        r_pos = lax.broadcasted_iota(jnp.int32, s_new.shape, 0) % t_new
        c_pos = lax.broadcasted_iota(jnp.int32, s_new.shape, 1)
        s_new = jnp.where(c_pos <= r_pos, s_new, -jnp.inf)
        _merge_softmax([_local_softmax(s_new, cb)], m_ref, l_ref, acc_ref)
        o = acc_ref[...] * (1.0 / l_ref[...])
        o_ref[...] = o.reshape(o_ref.shape)


def _decode_attend(q_lat, q_rope, ckv_new, kr_new, cache_ckv, cache_krope_t, page_table, *, t_new, n_pg):
    dec_batch, n_pages = page_table.shape
    n_tok = dec_batch * t_new
    assert n_pages % n_pg == 0 and dec_batch * (n_pages // n_pg) >= N_SLOTS
    pt_flat = page_table.reshape(-1)
    in_specs = [
        pl.BlockSpec((N_HEADS, t_new, KV_RANK), lambda b, j, pt: (0, b, 0)),
        pl.BlockSpec((N_HEADS, t_new, QK_ROPE), lambda b, j, pt: (0, b, 0)),
        pl.BlockSpec((t_new, KV_RANK), lambda b, j, pt: (b, 0)),
        pl.BlockSpec((t_new, QK_ROPE), lambda b, j, pt: (b, 0)),
        pl.BlockSpec(memory_space=pl.ANY),
        pl.BlockSpec(memory_space=pl.ANY),
    ]
    rows = N_HEADS * t_new
    grid_spec = pltpu.PrefetchScalarGridSpec(
        num_scalar_prefetch=1,
        grid=(dec_batch, n_pages // n_pg),
        in_specs=in_specs,
        out_specs=pl.BlockSpec((N_HEADS, t_new, KV_RANK), lambda b, j, pt: (0, b, 0)),
        scratch_shapes=[pltpu.VMEM((N_SLOTS, n_pg * PAGE_SIZE, KV_RANK), F32),
                        pltpu.VMEM((N_SLOTS, QK_ROPE, n_pg * PAGE_SIZE), F32),
                        pltpu.SemaphoreType.DMA((2, N_SLOTS)),
                        pltpu.VMEM((rows, 1), F32), pltpu.VMEM((rows, 1), F32),
                        pltpu.VMEM((rows, KV_RANK), F32)],
    )
    return pl.pallas_call(
        functools.partial(_decode_attn_kernel, n_pg),
        grid_spec=grid_spec,
        out_shape=jax.ShapeDtypeStruct((N_HEADS, n_tok, KV_RANK), F32),
        compiler_params=pltpu.CompilerParams(dimension_semantics=("arbitrary", "arbitrary"),
                                             vmem_limit_bytes=VMEM_LIMIT),
        name="decode_attend",
    )(pt_flat, q_lat, q_rope, ckv_new, kr_new, cache_ckv, cache_krope_t)


def _back_kernel(latent_attn, *refs):
    if latent_attn:
        (o_ref, u_ref, v_ref, x_ref, wuv_ref, wmix_ref, bmix_ref, ga_ref, gg_ref, wout_ref,
         g2_ref, wgate_ref, wup_ref, wdown_ref, gf_ref, y_ref) = refs
        o_flat = jnp.concatenate([o_ref[hd].astype(BF16) for hd in range(N_HEADS)], axis=1)
        a = _dot(o_flat, wuv_ref[...])
    else:
        (a_ref, u_ref, v_ref, x_ref, wmix_ref, bmix_ref, ga_ref, gg_ref, wout_ref,
         g2_ref, wgate_ref, wup_ref, wdown_ref, gf_ref, y_ref) = refs
        a = a_ref[...]
    tm = x_ref.shape[0]
    an = _rms(a, ga_ref[...])

    lane = lax.broadcasted_iota(jnp.int32, (CHUNK, GMLP_WIDTH), 1)
    wmix = wmix_ref[...]
    mixes = []
    for c in range(tm // CHUNK):
        vck = v_ref[c * CHUNK:(c + 1) * CHUNK, :].astype(BF16)
        bd = jnp.concatenate(
            [jnp.where((lane >= g * GROUP_DIM) & (lane < (g + 1) * GROUP_DIM), vck, jnp.zeros_like(vck))
             for g in range(N_GROUPS)], axis=0)
        mixes.append(_dot(wmix, bd) + bmix_ref[...])
    mix = mixes[0] if len(mixes) == 1 else jnp.concatenate(mixes, axis=0)
    gn = _rms(u_ref[...] * mix, gg_ref[...])

    merged = jnp.concatenate([an, gn], axis=1).astype(BF16)
    x1 = x_ref[...] + _dot(merged, wout_ref[...])
    h2 = _rms(x1, g2_ref[...]).astype(BF16)
    gate = _dot(h2, wgate_ref[...])
    up = _dot(h2, wup_ref[...])
    act = (gate * (1.0 / (1.0 + jnp.exp(-gate))) * up).astype(BF16)
    x2 = x1 + _dot(act, wdown_ref[...])
    y_ref[...] = _rms(x2, gf_ref[...])


def _back(attn, u, v, x, wts, *, tm, name):
    n_tok, d_model = x.shape
    latent_attn = attn.ndim == 3
    row = lambda w: pl.BlockSpec((tm, w), lambda i: (i, 0))
    attn_spec = pl.BlockSpec((N_HEADS, tm, KV_RANK), lambda i: (0, i, 0)) if latent_attn else row(MLA_WIDTH)
    in_specs = [attn_spec, row(GMLP_WIDTH), row(GMLP_WIDTH), row(d_model)]
    in_specs += [pl.BlockSpec(w.shape, lambda i, _nd=w.ndim: (0,) * _nd, pipeline_mode=pl.Buffered(1))
                 for w in wts]
    return pl.pallas_call(
        functools.partial(_back_kernel, latent_attn),
        grid=(n_tok // tm,),
        in_specs=in_specs,
        out_specs=row(d_model),
        out_shape=jax.ShapeDtypeStruct((n_tok, d_model), F32),
        compiler_params=pltpu.CompilerParams(dimension_semantics=("parallel",),
                                             vmem_limit_bytes=VMEM_LIMIT),
        name=name,
    )(attn, u, v, x, *wts)


def _rope_angles(pos):
    inv = ROPE_BASE ** (-jnp.arange(ROPE_HALF, dtype=F32) / ROPE_HALF)
    ang = pos.astype(F32)[:, None] * inv[None, :]
    return jnp.cos(ang), jnp.sin(ang)


def _lane_tiled(t, reps_rows):
    return jnp.tile(t, (reps_rows, LANES // ROPE_HALF))


def kernel(x_prompt, x_sample, cache_ckv, cache_krope, page_table, norm1_g, w_in, q_norm_g, w_q_up, kv_norm_g, w_uk, w_uv, v_norm_g, v_norm_b, w_spatial, b_spatial, out_norm_mla_g, out_norm_gmlp_g, w_out, norm2_g, w_gate, w_up, w_down, final_norm_g):
    batch, seq, d_model = x_prompt.shape
    dec_batch, t_new, _ = x_sample.shape
    depth = w_in.shape[0]
    past_len = page_table.shape[1] * PAGE_SIZE
    assert depth == 1 and seq % CHUNK == 0 and t_new <= CHUNK and CHUNK % t_new == 0

    xp = x_prompt.reshape(batch * seq, d_model)
    xs = x_sample.reshape(dec_batch * t_new, d_model)
    cos_p, sin_p = _rope_angles(jnp.arange(seq))
    cos_s, sin_s = _rope_angles(past_len + jnp.arange(t_new))
    tm_front, tm_tok, attn_blk, n_pg = 512, 256, 512, page_table.shape[1]
    assert tm_tok % t_new == 0 and seq % tm_front == 0 and seq % attn_blk == 0
    tables_p = (_lane_tiled(cos_p, 1), _lane_tiled(sin_p, 1), cos_p.T, sin_p.T)
    tables_s = (_lane_tiled(cos_s, tm_tok // t_new), _lane_tiled(sin_s, tm_tok // t_new))

    l = 0
    row2 = lambda g: g.reshape(1, -1)
    wi = w_in[l]
    off_kr = Q_RANK + KV_RANK
    off_g = off_kr + QK_ROPE
    w_in_p = jnp.concatenate(
        [wi[:, :off_kr], wi[:, off_g:], jnp.zeros((d_model, PK_R1), wi.dtype), wi[:, off_kr:off_g],
         jnp.zeros((d_model, LANES - PK_END), wi.dtype)], axis=1).astype(BF16)
    wq = w_q_up[l].reshape(Q_RANK, N_HEADS, QK_NOPE + QK_ROPE)
    wq_p = jnp.concatenate([wq, jnp.zeros((Q_RANK, N_HEADS, LANES - PK_END), wq.dtype)], axis=2)
    wq_p = wq_p.reshape(Q_RANK, N_HEADS * LANES).astype(BF16)
    wuk = jnp.transpose(w_uk[l], (1, 2, 0))
    wuk_h = jnp.concatenate([wuk, jnp.zeros((N_HEADS, LANES - QK_NOPE, KV_RANK), wuk.dtype)],
                            axis=1).astype(BF16)
    wuk_cols = jnp.transpose(wuk_h, (2, 0, 1)).reshape(KV_RANK, N_HEADS * LANES)
    front_tail = (row2(v_norm_g[l]), row2(v_norm_b[l]))

    eye_h = jnp.eye(N_HEADS, dtype=w_uv.dtype)
    wuv_bd = jnp.einsum('rhv,hg->hrgv', w_uv[l], eye_h).reshape(N_HEADS * KV_RANK, MLA_WIDTH).astype(BF16)
    tril = jnp.tril(jnp.ones((CHUNK, CHUNK), dtype=bool))
    wmix_p = jnp.where(tril, w_spatial[l], 0)
    bmix_p = jnp.repeat(b_spatial[l].T, GROUP_DIM, axis=1)
    reps = CHUNK // t_new
    tril_s = jnp.tril(jnp.ones((t_new, t_new), dtype=bool))
    w_small = jnp.where(tril_s, w_spatial[l][:, :t_new, :t_new], 0)
    wmix_s = jnp.einsum('ab,gij->gaibj', jnp.eye(reps, dtype=w_small.dtype), w_small)
    wmix_s = wmix_s.reshape(N_GROUPS, CHUNK, CHUNK)
    bmix_s = jnp.tile(jnp.repeat(b_spatial[l][:, :t_new].T, GROUP_DIM, axis=1), (reps, 1))
    cat_groups = lambda w: jnp.transpose(w, (1, 0, 2)).reshape(CHUNK, N_GROUPS * CHUNK).astype(BF16)
    back_tail = (row2(out_norm_mla_g[l]), row2(out_norm_gmlp_g[l]), w_out[l].astype(BF16), row2(norm2_g[l]),
                 w_gate[l].astype(BF16), w_up[l].astype(BF16), w_down[l].astype(BF16), row2(final_norm_g))

    wuv_t = w_uv[l].reshape(KV_RANK, MLA_WIDTH).T.astype(BF16)
    wts_p = (row2(norm1_g[l]), w_in_p, row2(q_norm_g[l]), wq_p.T, row2(kv_norm_g[l]), wuk_cols, wuv_t) + front_tail
    q_t, k_pk, ckv_p, v_t, kr_p, u_p, v_p = _front(xp, tables_p, wts_p, absorbed=False, tm=tm_front, seq=seq)
    a_p = _prompt_attend(k_pk, q_t, v_t, batch=batch, seq=seq, blk=attn_blk)
    y_p = _back(a_p, u_p, v_p, xp, (cat_groups(wmix_p), bmix_p) + back_tail, tm=tm_tok, name="back_prompt")

    wts_s = (row2(norm1_g[l]), w_in_p, row2(q_norm_g[l]), wq_p, row2(kv_norm_g[l]), wuk_h) + front_tail
    q_lat, q_rope, ckv_s, kr_s, u_s, v_s = _front(xs, tables_s, wts_s, absorbed=True, tm=tm_tok)
    o_s = _decode_attend(q_lat, q_rope, ckv_s, kr_s, cache_ckv, jnp.swapaxes(cache_krope, 2, 3), page_table,
                         t_new=t_new, n_pg=n_pg)
    y_s = _back(o_s, u_s, v_s, xs, (wuv_bd, cat_groups(wmix_s), bmix_s) + back_tail, tm=tm_tok, name="back_decode")

    return (y_p.reshape(batch, seq, d_model),
            y_s.reshape(dec_batch, t_new, d_model),
            ckv_p.reshape(depth, batch, seq, KV_RANK),
            kr_p.reshape(depth, batch, seq, QK_ROPE),
            ckv_s.reshape(depth, dec_batch, t_new, KV_RANK),
            kr_s.reshape(depth, dec_batch, t_new, QK_ROPE),
            v_s.reshape(depth, dec_batch, t_new, GMLP_WIDTH))
```

```python
import functools
import math

import jax
import jax.numpy as jnp
from jax import lax
from jax.experimental import pallas as pl
from jax.experimental.pallas import tpu as pltpu

LANES = 128
SUBLANES = 8
BF16_ROWS = 16
MXU_DIM = 256
VMEM_LIMIT = 56 * 1024 * 1024

N_HEADS = 8
QK_NOPE = 64
QK_ROPE = 32
ROPE_HALF = QK_ROPE // 2
Q_RANK = 384
KV_RANK = 256
V_HEAD = 64
N_GROUPS = 8
GROUP_DIM = 64
GMLP_WIDTH = N_GROUPS * GROUP_DIM
MLA_WIDTH = N_HEADS * V_HEAD
CHUNK = 128
PAGE_SIZE = 128
ROPE_BASE = 10000.0
EPS = 1e-6
Q_SCALE = (QK_NOPE + QK_ROPE) ** -0.5 * math.log2(math.e)

COL_Q = 0
COL_KV = COL_Q + Q_RANK
COL_U = COL_KV + KV_RANK
COL_V = COL_U + GMLP_WIDTH
COL_KR = COL_V + GMLP_WIDTH
IN_COLS_PAD = COL_KR + LANES
PK_R1 = QK_NOPE
PK_R2 = QK_NOPE + ROPE_HALF
PK_END = QK_NOPE + QK_ROPE

BF16 = jnp.bfloat16
F32 = jnp.float32


def _rms(x, g):
    return x * lax.rsqrt(jnp.mean(x * x, axis=-1, keepdims=True) + EPS) * g


def _dot(a, b):
    return jnp.dot(a, b, preferred_element_type=F32)


def _dot_nt(a, b):
    return lax.dot_general(a, b, (((1,), (1,)), ((), ())), preferred_element_type=F32)


def _rope_packed(t, c_mul, s_up, s_down):
    return (t * c_mul
            + pltpu.roll(t, LANES - ROPE_HALF, axis=1) * s_up
            + pltpu.roll(t, ROPE_HALF, axis=1) * s_down)


def _front_kernel(absorbed, *refs):
    if absorbed:
        (x_ref, cos_ref, sin_ref, g1_ref, w_in_ref, gq_ref, wq_ref, gkv_ref, wuk_ref, vg_ref, vb_ref,
         qlat_ref, qrope_ref, ckv_ref, kr_ref, u_ref, v_ref) = refs
    else:
        (x_ref, cos_ref, sin_ref, cost_ref, sint_ref, g1_ref, w_in_ref, gq_ref, wq_ref, gkv_ref, wuk_ref,
         wuvt_ref, vg_ref, vb_ref, qt_ref, kpk_ref, ckv_ref, vt_ref, kr_ref, u_ref, v_ref) = refs
    x = x_ref[...]
    h = _rms(x, g1_ref[...]).astype(BF16)
    z = _dot(h, w_in_ref[...])

    cos_t = cos_ref[...]
    sin_t = sin_ref[...]
    lane = lax.broadcasted_iota(jnp.int32, cos_t.shape, 1)
    in_r1 = (lane >= PK_R1) & (lane < PK_R2)
    in_r2 = (lane >= PK_R2) & (lane < PK_END)

    ckv = _rms(z[:, COL_KV:COL_U], gkv_ref[...])
    ckv_ref[...] = ckv
    k_c = jnp.where(in_r1 | in_r2, cos_t, 0.0)
    k_up = jnp.where(in_r1, -sin_t, 0.0)
    k_dn = jnp.where(in_r2, sin_t, 0.0)
    kr = _rope_packed(z[:, COL_KR:COL_KR + LANES], k_c, k_up, k_dn)
    kr_ref[...] = kr[:, PK_R1:PK_END]

    cqn = _rms(z[:, COL_Q:COL_KV], gq_ref[...]).astype(BF16)
    if absorbed:
        qp = _dot(cqn, wq_ref[...])
        q_c = jnp.where(lane < PK_R1, Q_SCALE, jnp.where(lane < PK_END, Q_SCALE * cos_t, 0.0))
        q_up = jnp.where(in_r1, -Q_SCALE * sin_t, 0.0)
        q_dn = jnp.where(in_r2, Q_SCALE * sin_t, 0.0)
        for hd in range(N_HEADS):
            qh = _rope_packed(qp[:, hd * LANES:(hd + 1) * LANES], q_c, q_up, q_dn)
            qlat_ref[hd] = _dot(qh.astype(BF16), wuk_ref[hd])
            qrope_ref[hd] = qh[:, PK_R1:PK_END]
    else:
        qpt = _dot_nt(wq_ref[...], cqn)
        c_t = cost_ref[...] * Q_SCALE
        s_t = sint_ref[...] * Q_SCALE
        zpad = jnp.zeros((LANES - PK_END, qpt.shape[1]), F32)
        for hd in range(N_HEADS):
            blk = qpt[hd * LANES:(hd + 1) * LANES]
            x1 = blk[PK_R1:PK_R2]
            x2 = blk[PK_R2:PK_END]
            qt = jnp.concatenate([blk[:PK_R1] * Q_SCALE, x1 * c_t - x2 * s_t, x1 * s_t + x2 * c_t, zpad],
                                 axis=0)
            qt_ref[hd] = qt.astype(BF16)
        ckv_b = ckv.astype(BF16)
        knope = _dot(ckv_b, wuk_ref[...])
        for hd in range(N_HEADS):
            kpk_ref[hd] = (knope[:, hd * LANES:(hd + 1) * LANES] + kr).astype(BF16)
        vt_ref[...] = _dot_nt(wuvt_ref[...], ckv_b).astype(BF16)

    zg = z[:, COL_U:COL_KR]
    zg = 0.5 * zg * (1.0 + lax.erf(zg * math.sqrt(0.5)))
    u_ref[...] = zg[:, :GMLP_WIDTH]
    vv = zg[:, GMLP_WIDTH:]
    mu = jnp.mean(vv, axis=-1, keepdims=True)
    vc = vv - mu
    v_ref[...] = vc * lax.rsqrt(jnp.mean(vc * vc, axis=-1, keepdims=True) + EPS) * vg_ref[...] + vb_ref[...]


def _const_spec(shape):
    nd = len(shape)
    return pl.BlockSpec(shape, lambda i, _nd=nd: (0,) * _nd)


def _front(x, tables, wts, *, absorbed, tm, seq=None):
    n_tok, d_model = x.shape
    grid = (n_tok // tm,)
    row = lambda w: pl.BlockSpec((tm, w), lambda i: (i, 0))
    head = lambda w, dt: (jax.ShapeDtypeStruct((N_HEADS, n_tok, w), dt),
                          pl.BlockSpec((N_HEADS, tm, w), lambda i: (0, i, 0)))
    flat = lambda w, dt: (jax.ShapeDtypeStruct((n_tok, w), dt), row(w))
    tab_blocks = tables[0].shape[0] // tm
    table_specs = [pl.BlockSpec((tm, LANES), lambda i: (i % tab_blocks, 0))] * 2
    if absorbed:
        outs = [head(KV_RANK, F32), head(QK_ROPE, F32), flat(KV_RANK, F32), flat(QK_ROPE, F32),
                flat(GMLP_WIDTH, F32), flat(GMLP_WIDTH, F32)]
    else:
        spb = seq // tm
        nb = n_tok // seq
        table_specs += [pl.BlockSpec((ROPE_HALF, tm), lambda i: (0, i % spb))] * 2
        qt = (jax.ShapeDtypeStruct((nb, N_HEADS, LANES, seq), BF16),
              pl.BlockSpec((None, N_HEADS, LANES, tm), lambda i: (i // spb, 0, 0, i % spb)))
        vt = (jax.ShapeDtypeStruct((nb, MLA_WIDTH, seq), BF16),
              pl.BlockSpec((None, MLA_WIDTH, tm), lambda i: (i // spb, 0, i % spb)))
        outs = [qt, head(LANES, BF16), flat(KV_RANK, F32), vt, flat(QK_ROPE, F32),
                flat(GMLP_WIDTH, F32), flat(GMLP_WIDTH, F32)]
    ins = [x] + list(tables) + list(wts)
    in_specs = [row(d_model)] + table_specs + [_const_spec(w.shape) for w in wts]
    return pl.pallas_call(
        functools.partial(_front_kernel, absorbed),
        grid=grid,
        in_specs=in_specs,
        out_specs=[o[1] for o in outs],
        out_shape=[o[0] for o in outs],
        compiler_params=pltpu.CompilerParams(dimension_semantics=("parallel",),
                                             vmem_limit_bytes=VMEM_LIMIT),
        name="front_absorbed" if absorbed else "front_prompt",
    )(*ins)


ATTN_UNIT_COLS = 2 * MXU_DIM
ATTN_UNIT_KEYS = 256
SOFTMAX_ROW_CHUNKS = 4
def _prompt_attn_kernel(qi_ref, ki_ref, k_ref, qt_ref, vt_ref, a_ref, m_ref, l_ref, acc_ref):
    p_id = pl.program_id(1)
    qi = qi_ref[p_id]
    ki = ki_ref[p_id]
    blk_k = k_ref.shape[1]
    blk_q = qt_ref.shape[2]
    qw = ATTN_UNIT_COLS

    @pl.when(ki == 0)
    def _():
        m_ref[...] = jnp.full(m_ref.shape, -jnp.inf, F32)
        l_ref[...] = jnp.zeros(l_ref.shape, F32)
        acc_ref[...] = jnp.zeros(acc_ref.shape, F32)

    def step(diagonal):
        kw = min(ATTN_UNIT_KEYS, blk_k)
        units = [(k0, c, hd) for k0 in range(0, blk_k, kw) for c in range(blk_q // qw)
                 if not (diagonal and k0 >= (c + 1) * qw) for hd in range(N_HEADS)]

        def scores(k0, c, hd):
            st = _dot(k_ref[hd, k0:k0 + kw, :], qt_ref[hd, :, c * qw:(c + 1) * qw])
            if diagonal:
                kpos = lax.broadcasted_iota(jnp.int32, (kw, qw), 0) + k0
                qpos = lax.broadcasted_iota(jnp.int32, (kw, qw), 1) + c * qw
                st = jnp.where(kpos <= qpos, st, -jnp.inf)
            return st

        def rows(st, r):
            rk = st.shape[0] // SOFTMAX_ROW_CHUNKS
            return st[r * rk:(r + 1) * rk]

        def fold8(x, op):
            return op(x.reshape(x.shape[0] // SUBLANES, SUBLANES, x.shape[1]), axis=0)

        n_units = len(units)
        st_cur = scores(*units[0])
        mloc_cur = jnp.max(st_cur, axis=0, keepdims=True)
        st_nxt = scores(*units[1])
        for idx, (k0, c, hd) in enumerate(units):
            cols = slice(c * qw, (c + 1) * qw)
            m_prev = m_ref[hd, :, cols]
            m_new = jnp.maximum(m_prev, mloc_cur)
            alpha = jnp.exp2(m_prev - m_new)
            st_nn = scores(*units[idx + 2]) if idx + 2 < n_units else None
            mx8 = None
            pts = []
            for r in range(SOFTMAX_ROW_CHUNKS):
                if idx + 1 < n_units:
                    part = fold8(rows(st_nxt, r), jnp.max)
                    mx8 = part if mx8 is None else jnp.maximum(mx8, part)
                pts.append(jnp.exp2(rows(st_cur, r) - m_new).astype(BF16))
            pt = jnp.concatenate(pts, axis=0)
            mloc_nxt = None if mx8 is None else jnp.max(mx8, axis=0, keepdims=True)
            vt = jnp.concatenate([vt_ref[hd * V_HEAD:(hd + 1) * V_HEAD, k0:k0 + kw],
                                  jnp.ones((BF16_ROWS, kw), BF16)], axis=0)
            pv = _dot(vt, pt)
            l_ref[hd, :, cols] = alpha * l_ref[hd, :, cols] + pv[V_HEAD:V_HEAD + 1]
            acc_ref[hd, :, cols] = alpha * acc_ref[hd, :, cols] + pv[:V_HEAD]
            m_ref[hd, :, cols] = m_new
            st_cur, mloc_cur, st_nxt = st_nxt, mloc_nxt, st_nn

    @pl.when(ki < qi)
    def _():
        step(False)

    @pl.when(ki == qi)
    def _():
        step(True)
        at = jnp.concatenate([acc_ref[hd] * (1.0 / l_ref[hd]) for hd in range(N_HEADS)], axis=0)
        a_ref[...] = jnp.transpose(at)


def _prompt_attend(k_pk, q_t, v_t, *, batch, seq, blk):
    nq = seq // blk
    pairs = [(q, k) for q in range(nq) for k in range(q + 1)]
    qi = jnp.asarray([p[0] for p in pairs], jnp.int32)
    ki = jnp.asarray([p[1] for p in pairs], jnp.int32)
    n_tok = batch * seq
    grid_spec = pltpu.PrefetchScalarGridSpec(
        num_scalar_prefetch=2,
        grid=(batch, len(pairs)),
        in_specs=[
            pl.BlockSpec((N_HEADS, blk, LANES), lambda b, p, qi, ki: (0, b * nq + ki[p], 0)),
            pl.BlockSpec((None, N_HEADS, LANES, blk), lambda b, p, qi, ki: (b, 0, 0, qi[p])),
            pl.BlockSpec((None, MLA_WIDTH, blk), lambda b, p, qi, ki: (b, 0, ki[p])),
        ],
        out_specs=pl.BlockSpec((blk, MLA_WIDTH), lambda b, p, qi, ki: (b * nq + qi[p], 0)),
        scratch_shapes=[pltpu.VMEM((N_HEADS, 1, blk), F32), pltpu.VMEM((N_HEADS, 1, blk), F32),
                        pltpu.VMEM((N_HEADS, V_HEAD, blk), F32)],
    )
    return pl.pallas_call(
        _prompt_attn_kernel,
        grid_spec=grid_spec,
        out_shape=jax.ShapeDtypeStruct((n_tok, MLA_WIDTH), F32),
        compiler_params=pltpu.CompilerParams(dimension_semantics=("parallel", "arbitrary"),
                                             vmem_limit_bytes=VMEM_LIMIT),
        name="prompt_attend",
    )(qi, ki, k_pk, q_t, v_t)


N_SLOTS = 3
DECODE_SPLIT = 4


def _local_softmax(s, v_b):
    m = jnp.max(s, axis=-1, keepdims=True)
    p = jnp.exp2(s - m)
    return m, jnp.sum(p, axis=-1, keepdims=True), _dot(p.astype(BF16), v_b)


def _merge_softmax(parts, m_ref, l_ref, acc_ref):
    m_run = m_ref[...]
    m_new = m_run
    for m, _, _ in parts:
        m_new = jnp.maximum(m_new, m)
    w_run = jnp.exp2(m_run - m_new)
    l_new = w_run * l_ref[...]
    acc_new = w_run * acc_ref[...]
    for m, l, acc in parts:
        w = jnp.exp2(m - m_new)
        l_new = l_new + w * l
        acc_new = acc_new + w * acc
    m_ref[...] = m_new
    l_ref[...] = l_new
    acc_ref[...] = acc_new


def _decode_attn_kernel(n_pg, pt_ref, ql_ref, qr_ref, cn_ref, rn_ref, ckv_hbm, krt_hbm, o_ref,
                        kbuf, rbuf, sems, m_ref, l_ref, acc_ref):
    b = pl.program_id(0)
    j = pl.program_id(1)
    nj = pl.num_programs(1)
    chunk = b * nj + j
    n_chunks = pl.num_programs(0) * nj
    slot = chunk % N_SLOTS

    def page_copies(ck, sl, i):
        page = pt_ref[ck * n_pg + i]
        rows = pl.ds(i * PAGE_SIZE, PAGE_SIZE)
        return (pltpu.make_async_copy(ckv_hbm.at[0, page], kbuf.at[sl, rows], sems.at[0, sl]),
                pltpu.make_async_copy(krt_hbm.at[0, page], rbuf.at[sl, :, rows], sems.at[1, sl]))

    def start_chunk(ck, sl):
        for i in range(n_pg):
            for cp in page_copies(ck, sl, i):
                cp.start()

    ahead = N_SLOTS - 1

    @pl.when(chunk == 0)
    def _():
        for ck in range(ahead):
            start_chunk(ck, ck)

    @pl.when(chunk + ahead < n_chunks)
    def _():
        start_chunk(chunk + ahead, (chunk + ahead) % N_SLOTS)

    @pl.when(j == 0)
    def _():
        m_ref[...] = jnp.full(m_ref.shape, -jnp.inf, F32)
        l_ref[...] = jnp.zeros(l_ref.shape, F32)
        acc_ref[...] = jnp.zeros(acc_ref.shape, F32)

    n_rows = N_HEADS * ql_ref.shape[1]
    ql = ql_ref[...].reshape(n_rows, KV_RANK).astype(BF16)
    qr = qr_ref[...].reshape(n_rows, QK_ROPE).astype(BF16)

    for i in range(n_pg):
        for cp in page_copies(chunk, slot, i):
            cp.wait()

    keys = n_pg * PAGE_SIZE // DECODE_SPLIT
    kbs, scores = [], []
    for c in range(DECODE_SPLIT):
        kb = kbuf[slot, c * keys:(c + 1) * keys, :].astype(BF16)
        rbt = rbuf[slot, :, c * keys:(c + 1) * keys].astype(BF16)
        kbs.append(kb)
        scores.append(_dot_nt(ql, kb) + _dot(qr, rbt))
    parts = [_local_softmax(s, kb) for s, kb in zip(scores, kbs)]
    _merge_softmax(parts, m_ref, l_ref, acc_ref)

    @pl.when(j == nj - 1)
    def _():
        t_new = cn_ref.shape[0]
        cb = cn_ref[...].astype(BF16)
        s_new = _dot_nt(ql, cb) + _dot_nt(qr, rn_ref[...].astype(BF16))
        r_pos = lax.broadcasted_iota(jnp.int32, s_new.shape, 0) % t_new
        c_pos = lax.broadcasted_iota(jnp.int32, s_new.shape, 1)
        s_new = jnp.where(c_pos <= r_pos, s_new, -jnp.inf)
        _merge_softmax([_local_softmax(s_new, cb)], m_ref, l_ref, acc_ref)
        o = acc_ref[...] * (1.0 / l_ref[...])
        o_ref[...] = o.reshape(o_ref.shape)


def _decode_attend(q_lat, q_rope, ckv_new, kr_new, cache_ckv, cache_krope_t, page_table, *, t_new, n_pg):
    dec_batch, n_pages = page_table.shape
    n_tok = dec_batch * t_new
    assert n_pages % n_pg == 0 and dec_batch * (n_pages // n_pg) >= N_SLOTS
    pt_flat = page_table.reshape(-1)
    in_specs = [
        pl.BlockSpec((N_HEADS, t_new, KV_RANK), lambda b, j, pt: (0, b, 0)),
        pl.BlockSpec((N_HEADS, t_new, QK_ROPE), lambda b, j, pt: (0, b, 0)),
        pl.BlockSpec((t_new, KV_RANK), lambda b, j, pt: (b, 0)),
        pl.BlockSpec((t_new, QK_ROPE), lambda b, j, pt: (b, 0)),
        pl.BlockSpec(memory_space=pl.ANY),
        pl.BlockSpec(memory_space=pl.ANY),
    ]
    rows = N_HEADS * t_new
    grid_spec = pltpu.PrefetchScalarGridSpec(
        num_scalar_prefetch=1,
        grid=(dec_batch, n_pages // n_pg),
        in_specs=in_specs,
        out_specs=pl.BlockSpec((N_HEADS, t_new, KV_RANK), lambda b, j, pt: (0, b, 0)),
        scratch_shapes=[pltpu.VMEM((N_SLOTS, n_pg * PAGE_SIZE, KV_RANK), F32),
                        pltpu.VMEM((N_SLOTS, QK_ROPE, n_pg * PAGE_SIZE), F32),
                        pltpu.SemaphoreType.DMA((2, N_SLOTS)),
                        pltpu.VMEM((rows, 1), F32), pltpu.VMEM((rows, 1), F32),
                        pltpu.VMEM((rows, KV_RANK), F32)],
    )
    return pl.pallas_call(
        functools.partial(_decode_attn_kernel, n_pg),
        grid_spec=grid_spec,
        out_shape=jax.ShapeDtypeStruct((N_HEADS, n_tok, KV_RANK), F32),
        compiler_params=pltpu.CompilerParams(dimension_semantics=("arbitrary", "arbitrary"),
                                             vmem_limit_bytes=VMEM_LIMIT),
        name="decode_attend",
    )(pt_flat, q_lat, q_rope, ckv_new, kr_new, cache_ckv, cache_krope_t)


def _back_kernel(latent_attn, *refs):
    if latent_attn:
        (o_ref, u_ref, v_ref, x_ref, wuv_ref, wmix_ref, bmix_ref, ga_ref, gg_ref, wout_ref,
         g2_ref, wgate_ref, wup_ref, wdown_ref, gf_ref, y_ref) = refs
        o_flat = jnp.concatenate([o_ref[hd].astype(BF16) for hd in range(N_HEADS)], axis=1)
        a = _dot(o_flat, wuv_ref[...])
    else:
        (a_ref, u_ref, v_ref, x_ref, wmix_ref, bmix_ref, ga_ref, gg_ref, wout_ref,
         g2_ref, wgate_ref, wup_ref, wdown_ref, gf_ref, y_ref) = refs
        a = a_ref[...]
    tm = x_ref.shape[0]
    an = _rms(a, ga_ref[...])

    lane = lax.broadcasted_iota(jnp.int32, (CHUNK, GMLP_WIDTH), 1)
    wmix = wmix_ref[...]
    mixes = []
    for c in range(tm // CHUNK):
        vck = v_ref[c * CHUNK:(c + 1) * CHUNK, :].astype(BF16)
        bd = jnp.concatenate(
            [jnp.where((lane >= g * GROUP_DIM) & (lane < (g + 1) * GROUP_DIM), vck, jnp.zeros_like(vck))
             for g in range(N_GROUPS)], axis=0)
        mixes.append(_dot(wmix, bd) + bmix_ref[...])
    mix = mixes[0] if len(mixes) == 1 else jnp.concatenate(mixes, axis=0)
    gn = _rms(u_ref[...] * mix, gg_ref[...])

    merged = jnp.concatenate([an, gn], axis=1).astype(BF16)
    x1 = x_ref[...] + _dot(merged, wout_ref[...])
    h2 = _rms(x1, g2_ref[...]).astype(BF16)
    gate = _dot(h2, wgate_ref[...])
    up = _dot(h2, wup_ref[...])
    act = (gate * (1.0 / (1.0 + jnp.exp(-gate))) * up).astype(BF16)
    x2 = x1 + _dot(act, wdown_ref[...])
    y_ref[...] = _rms(x2, gf_ref[...])


def _back(attn, u, v, x, wts, *, tm, name):
    n_tok, d_model = x.shape
    latent_attn = attn.ndim == 3
    row = lambda w: pl.BlockSpec((tm, w), lambda i: (i, 0))
    attn_spec = pl.BlockSpec((N_HEADS, tm, KV_RANK), lambda i: (0, i, 0)) if latent_attn else row(MLA_WIDTH)
    in_specs = [attn_spec, row(GMLP_WIDTH), row(GMLP_WIDTH), row(d_model)]
    in_specs += [pl.BlockSpec(w.shape, lambda i, _nd=w.ndim: (0,) * _nd, pipeline_mode=pl.Buffered(1))
                 for w in wts]
    return pl.pallas_call(
        functools.partial(_back_kernel, latent_attn),
        grid=(n_tok // tm,),
        in_specs=in_specs,
        out_specs=row(d_model),
        out_shape=jax.ShapeDtypeStruct((n_tok, d_model), F32),
        compiler_params=pltpu.CompilerParams(dimension_semantics=("parallel",),
                                             vmem_limit_bytes=VMEM_LIMIT),
        name=name,
    )(attn, u, v, x, *wts)


def _rope_angles(pos):
    inv = ROPE_BASE ** (-jnp.arange(ROPE_HALF, dtype=F32) / ROPE_HALF)
    ang = pos.astype(F32)[:, None] * inv[None, :]
    return jnp.cos(ang), jnp.sin(ang)


def _lane_tiled(t, reps_rows):
    return jnp.tile(t, (reps_rows, LANES // ROPE_HALF))


def kernel(x_prompt, x_sample, cache_ckv, cache_krope, page_table, norm1_g, w_in, q_norm_g, w_q_up, kv_norm_g, w_uk, w_uv, v_norm_g, v_norm_b, w_spatial, b_spatial, out_norm_mla_g, out_norm_gmlp_g, w_out, norm2_g, w_gate, w_up, w_down, final_norm_g):
    batch, seq, d_model = x_prompt.shape
    dec_batch, t_new, _ = x_sample.shape
    depth = w_in.shape[0]
    past_len = page_table.shape[1] * PAGE_SIZE
    assert depth == 1 and seq % CHUNK == 0 and t_new <= CHUNK and CHUNK % t_new == 0

    xp = x_prompt.reshape(batch * seq, d_model)
    xs = x_sample.reshape(dec_batch * t_new, d_model)
    cos_p, sin_p = _rope_angles(jnp.arange(seq))
    cos_s, sin_s = _rope_angles(past_len + jnp.arange(t_new))
    tm_front, tm_tok, attn_blk, n_pg = 512, 256, 512, page_table.shape[1]
    assert tm_tok % t_new == 0 and seq % tm_front == 0 and seq % attn_blk == 0
    tables_p = (_lane_tiled(cos_p, 1), _lane_tiled(sin_p, 1), cos_p.T, sin_p.T)
    tables_s = (_lane_tiled(cos_s, tm_tok // t_new), _lane_tiled(sin_s, tm_tok // t_new))

    l = 0
    row2 = lambda g: g.reshape(1, -1)
    wi = w_in[l]
    off_kr = Q_RANK + KV_RANK
    off_g = off_kr + QK_ROPE
    w_in_p = jnp.concatenate(
        [wi[:, :off_kr], wi[:, off_g:], jnp.zeros((d_model, PK_R1), wi.dtype), wi[:, off_kr:off_g],
         jnp.zeros((d_model, LANES - PK_END), wi.dtype)], axis=1).astype(BF16)
    wq = w_q_up[l].reshape(Q_RANK, N_HEADS, QK_NOPE + QK_ROPE)
    wq_p = jnp.concatenate([wq, jnp.zeros((Q_RANK, N_HEADS, LANES - PK_END), wq.dtype)], axis=2)
    wq_p = wq_p.reshape(Q_RANK, N_HEADS * LANES).astype(BF16)
    wuk = jnp.transpose(w_uk[l], (1, 2, 0))
    wuk_h = jnp.concatenate([wuk, jnp.zeros((N_HEADS, LANES - QK_NOPE, KV_RANK), wuk.dtype)],
                            axis=1).astype(BF16)
    wuk_cols = jnp.transpose(wuk_h, (2, 0, 1)).reshape(KV_RANK, N_HEADS * LANES)
    front_tail = (row2(v_norm_g[l]), row2(v_norm_b[l]))

    eye_h = jnp.eye(N_HEADS, dtype=w_uv.dtype)
    wuv_bd = jnp.einsum('rhv,hg->hrgv', w_uv[l], eye_h).reshape(N_HEADS * KV_RANK, MLA_WIDTH).astype(BF16)
    tril = jnp.tril(jnp.ones((CHUNK, CHUNK), dtype=bool))
    wmix_p = jnp.where(tril, w_spatial[l], 0)
    bmix_p = jnp.repeat(b_spatial[l].T, GROUP_DIM, axis=1)
    reps = CHUNK // t_new
    tril_s = jnp.tril(jnp.ones((t_new, t_new), dtype=bool))
    w_small = jnp.where(tril_s, w_spatial[l][:, :t_new, :t_new], 0)
    wmix_s = jnp.einsum('ab,gij->gaibj', jnp.eye(reps, dtype=w_small.dtype), w_small)
    wmix_s = wmix_s.reshape(N_GROUPS, CHUNK, CHUNK)
    bmix_s = jnp.tile(jnp.repeat(b_spatial[l][:, :t_new].T, GROUP_DIM, axis=1), (reps, 1))
    cat_groups = lambda w: jnp.transpose(w, (1, 0, 2)).reshape(CHUNK, N_GROUPS * CHUNK).astype(BF16)
    back_tail = (row2(out_norm_mla_g[l]), row2(out_norm_gmlp_g[l]), w_out[l].astype(BF16), row2(norm2_g[l]),
                 w_gate[l].astype(BF16), w_up[l].astype(BF16), w_down[l].astype(BF16), row2(final_norm_g))

    wuv_t = w_uv[l].reshape(KV_RANK, MLA_WIDTH).T.astype(BF16)
    wts_p = (row2(norm1_g[l]), w_in_p, row2(q_norm_g[l]), wq_p.T, row2(kv_norm_g[l]), wuk_cols, wuv_t) + front_tail
    q_t, k_pk, ckv_p, v_t, kr_p, u_p, v_p = _front(xp, tables_p, wts_p, absorbed=False, tm=tm_front, seq=seq)
    a_p = _prompt_attend(k_pk, q_t, v_t, batch=batch, seq=seq, blk=attn_blk)
    y_p = _back(a_p, u_p, v_p, xp, (cat_groups(wmix_p), bmix_p) + back_tail, tm=tm_tok, name="back_prompt")

    wts_s = (row2(norm1_g[l]), w_in_p, row2(q_norm_g[l]), wq_p, row2(kv_norm_g[l]), wuk_h) + front_tail
    q_lat, q_rope, ckv_s, kr_s, u_s, v_s = _front(xs, tables_s, wts_s, absorbed=True, tm=tm_tok)
    o_s = _decode_attend(q_lat, q_rope, ckv_s, kr_s, cache_ckv, jnp.swapaxes(cache_krope, 2, 3), page_table,
                         t_new=t_new, n_pg=n_pg)
    y_s = _back(o_s, u_s, v_s, xs, (wuv_bd, cat_groups(wmix_s), bmix_s) + back_tail, tm=tm_tok, name="back_decode")

    return (y_p.reshape(batch, seq, d_model),
            y_s.reshape(dec_batch, t_new, d_model),
            ckv_p.reshape(depth, batch, seq, KV_RANK),
            kr_p.reshape(depth, batch, seq, QK_ROPE),
            ckv_s.reshape(depth, dec_batch, t_new, KV_RANK),
            kr_s.reshape(depth, dec_batch, t_new, QK_ROPE),
            v_s.reshape(depth, dec_batch, t_new, GMLP_WIDTH))
```

```python
import functools
import math

import jax
import jax.numpy as jnp
from jax import lax
from jax.experimental import pallas as pl
from jax.experimental.pallas import tpu as pltpu

LANES = 128
SUBLANES = 8
BF16_ROWS = 16
MXU_DIM = 256
VMEM_LIMIT = 56 * 1024 * 1024

N_HEADS = 8
QK_NOPE = 64
QK_ROPE = 32
ROPE_HALF = QK_ROPE // 2
Q_RANK = 384
KV_RANK = 256
V_HEAD = 64
N_GROUPS = 8
GROUP_DIM = 64
GMLP_WIDTH = N_GROUPS * GROUP_DIM
MLA_WIDTH = N_HEADS * V_HEAD
CHUNK = 128
PAGE_SIZE = 128
ROPE_BASE = 10000.0
EPS = 1e-6
Q_SCALE = (QK_NOPE + QK_ROPE) ** -0.5 * math.log2(math.e)

COL_Q = 0
COL_KV = COL_Q + Q_RANK
COL_U = COL_KV + KV_RANK
COL_V = COL_U + GMLP_WIDTH
COL_KR = COL_V + GMLP_WIDTH
IN_COLS_PAD = COL_KR + LANES
PK_R1 = QK_NOPE
PK_R2 = QK_NOPE + ROPE_HALF
PK_END = QK_NOPE + QK_ROPE

BF16 = jnp.bfloat16
F32 = jnp.float32


def _rms(x, g):
    return x * lax.rsqrt(jnp.mean(x * x, axis=-1, keepdims=True) + EPS) * g


def _dot(a, b):
    return jnp.dot(a, b, preferred_element_type=F32)


def _dot_nt(a, b):
    return lax.dot_general(a, b, (((1,), (1,)), ((), ())), preferred_element_type=F32)


def _rope_packed(t, c_mul, s_up, s_down):
    return (t * c_mul
            + pltpu.roll(t, LANES - ROPE_HALF, axis=1) * s_up
            + pltpu.roll(t, ROPE_HALF, axis=1) * s_down)


def _front_kernel(absorbed, *refs):
    if absorbed:
        (x_ref, cos_ref, sin_ref, g1_ref, w_in_ref, gq_ref, wq_ref, gkv_ref, wuk_ref, vg_ref, vb_ref,
         qlat_ref, qrope_ref, ckv_ref, kr_ref, u_ref, v_ref) = refs
    else:
        (x_ref, cos_ref, sin_ref, cost_ref, sint_ref, g1_ref, w_in_ref, gq_ref, wq_ref, gkv_ref, wuk_ref,
         wuvt_ref, vg_ref, vb_ref, qt_ref, kpk_ref, ckv_ref, vt_ref, kr_ref, u_ref, v_ref) = refs
    x = x_ref[...]
    h = _rms(x, g1_ref[...]).astype(BF16)
    z = _dot(h, w_in_ref[...])

    cos_t = cos_ref[...]
    sin_t = sin_ref[...]
    lane = lax.broadcasted_iota(jnp.int32, cos_t.shape, 1)
    in_r1 = (lane >= PK_R1) & (lane < PK_R2)
    in_r2 = (lane >= PK_R2) & (lane < PK_END)

    ckv = _rms(z[:, COL_KV:COL_U], gkv_ref[...])
    ckv_ref[...] = ckv
    k_c = jnp.where(in_r1 | in_r2, cos_t, 0.0)
    k_up = jnp.where(in_r1, -sin_t, 0.0)
    k_dn = jnp.where(in_r2, sin_t, 0.0)
    kr = _rope_packed(z[:, COL_KR:COL_KR + LANES], k_c, k_up, k_dn)
    kr_ref[...] = kr[:, PK_R1:PK_END]

    cqn = _rms(z[:, COL_Q:COL_KV], gq_ref[...]).astype(BF16)
    if absorbed:
        qp = _dot(cqn, wq_ref[...])
        q_c = jnp.where(lane < PK_R1, Q_SCALE, jnp.where(lane < PK_END, Q_SCALE * cos_t, 0.0))
        q_up = jnp.where(in_r1, -Q_SCALE * sin_t, 0.0)
        q_dn = jnp.where(in_r2, Q_SCALE * sin_t, 0.0)
        for hd in range(N_HEADS):
            qh = _rope_packed(qp[:, hd * LANES:(hd + 1) * LANES], q_c, q_up, q_dn)
            qlat_ref[hd] = _dot(qh.astype(BF16), wuk_ref[hd])
            qrope_ref[hd] = qh[:, PK_R1:PK_END]
    else:
        qpt = _dot_nt(wq_ref[...], cqn)
        c_t = cost_ref[...] * Q_SCALE
        s_t = sint_ref[...] * Q_SCALE
        zpad = jnp.zeros((LANES - PK_END, qpt.shape[1]), F32)
        for hd in range(N_HEADS):
            blk = qpt[hd * LANES:(hd + 1) * LANES]
            x1 = blk[PK_R1:PK_R2]
            x2 = blk[PK_R2:PK_END]
            qt = jnp.concatenate([blk[:PK_R1] * Q_SCALE, x1 * c_t - x2 * s_t, x1 * s_t + x2 * c_t, zpad],
                                 axis=0)
            qt_ref[hd] = qt.astype(BF16)
        ckv_b = ckv.astype(BF16)
        knope = _dot(ckv_b, wuk_ref[...])
        for hd in range(N_HEADS):
            kpk_ref[hd] = (knope[:, hd * LANES:(hd + 1) * LANES] + kr).astype(BF16)
        vt_ref[...] = _dot_nt(wuvt_ref[...], ckv_b).astype(BF16)

    zg = z[:, COL_U:COL_KR]
    zg = 0.5 * zg * (1.0 + lax.erf(zg * math.sqrt(0.5)))
    u_ref[...] = zg[:, :GMLP_WIDTH]
    vv = zg[:, GMLP_WIDTH:]
    mu = jnp.mean(vv, axis=-1, keepdims=True)
    vc = vv - mu
    v_ref[...] = vc * lax.rsqrt(jnp.mean(vc * vc, axis=-1, keepdims=True) + EPS) * vg_ref[...] + vb_ref[...]


def _const_spec(shape):
    nd = len(shape)
    return pl.BlockSpec(shape, lambda i, _nd=nd: (0,) * _nd)


def _front(x, tables, wts, *, absorbed, tm, seq=None):
    n_tok, d_model = x.shape
    grid = (n_tok // tm,)
    row = lambda w: pl.BlockSpec((tm, w), lambda i: (i, 0))
    head = lambda w, dt: (jax.ShapeDtypeStruct((N_HEADS, n_tok, w), dt),
                          pl.BlockSpec((N_HEADS, tm, w), lambda i: (0, i, 0)))
    flat = lambda w, dt: (jax.ShapeDtypeStruct((n_tok, w), dt), row(w))
    tab_blocks = tables[0].shape[0] // tm
    table_specs = [pl.BlockSpec((tm, LANES), lambda i: (i % tab_blocks, 0))] * 2
    if absorbed:
        outs = [head(KV_RANK, F32), head(QK_ROPE, F32), flat(KV_RANK, F32), flat(QK_ROPE, F32),
                flat(GMLP_WIDTH, F32), flat(GMLP_WIDTH, F32)]
    else:
        spb = seq // tm
        nb = n_tok // seq
        table_specs += [pl.BlockSpec((ROPE_HALF, tm), lambda i: (0, i % spb))] * 2
        qt = (jax.ShapeDtypeStruct((nb, N_HEADS, LANES, seq), BF16),
              pl.BlockSpec((None, N_HEADS, LANES, tm), lambda i: (i // spb, 0, 0, i % spb)))
        vt = (jax.ShapeDtypeStruct((nb, MLA_WIDTH, seq), BF16),
              pl.BlockSpec((None, MLA_WIDTH, tm), lambda i: (i // spb, 0, i % spb)))
        outs = [qt, head(LANES, BF16), flat(KV_RANK, F32), vt, flat(QK_ROPE, F32),
                flat(GMLP_WIDTH, F32), flat(GMLP_WIDTH, F32)]
    ins = [x] + list(tables) + list(wts)
    in_specs = [row(d_model)] + table_specs + [_const_spec(w.shape) for w in wts]
    return pl.pallas_call(
        functools.partial(_front_kernel, absorbed),
        grid=grid,
        in_specs=in_specs,
        out_specs=[o[1] for o in outs],
        out_shape=[o[0] for o in outs],
        compiler_params=pltpu.CompilerParams(dimension_semantics=("parallel",),
                                             vmem_limit_bytes=VMEM_LIMIT),
        name="front_absorbed" if absorbed else "front_prompt",
    )(*ins)


ATTN_UNIT_COLS = 2 * MXU_DIM
ATTN_UNIT_KEYS = 256
SOFTMAX_ROW_CHUNKS = 4
def _prompt_attn_kernel(qi_ref, ki_ref, kind_ref, k_ref, qt_ref, vt_ref, a_ref, m_ref, l_ref, acc_ref):
    p_id = pl.program_id(1)
    ki = ki_ref[p_id]
    kind = kind_ref[p_id]
    blk_k = k_ref.shape[1]
    blk_q = qt_ref.shape[2]
    qw = ATTN_UNIT_COLS

    @pl.when(ki == 0)
    def _():
        m_ref[...] = jnp.full(m_ref.shape, -jnp.inf, F32)
        l_ref[...] = jnp.zeros(l_ref.shape, F32)
        acc_ref[...] = jnp.zeros(acc_ref.shape, F32)

    def step(delta):
        kw = min(ATTN_UNIT_KEYS, blk_k)
        first_q = lambda c: delta + c * qw
        units = [(k0, c, hd) for k0 in range(0, blk_k, kw) for c in range(blk_q // qw)
                 if delta is None or k0 < first_q(c) + qw for hd in range(N_HEADS)]

        def scores(k0, c, hd):
            st = _dot(k_ref[hd, k0:k0 + kw, :], qt_ref[hd, :, c * qw:(c + 1) * qw])
            if delta is not None and k0 + kw - 1 > first_q(c):
                kpos = lax.broadcasted_iota(jnp.int32, (kw, qw), 0) + k0
                qpos = lax.broadcasted_iota(jnp.int32, (kw, qw), 1) + first_q(c)
                st = jnp.where(kpos <= qpos, st, -jnp.inf)
            return st

        def rows(st, r):
            rk = st.shape[0] // SOFTMAX_ROW_CHUNKS
            return st[r * rk:(r + 1) * rk]

        def fold8(x, op):
            return op(x.reshape(x.shape[0] // SUBLANES, SUBLANES, x.shape[1]), axis=0)

        n_units = len(units)
        st_cur = scores(*units[0])
        mloc_cur = jnp.max(st_cur, axis=0, keepdims=True)
        st_nxt = scores(*units[1])
        for idx, (k0, c, hd) in enumerate(units):
            cols = slice(c * qw, (c + 1) * qw)
            m_prev = m_ref[hd, :, cols]
            m_new = jnp.maximum(m_prev, mloc_cur)
            alpha = jnp.exp2(m_prev - m_new)
            st_nn = scores(*units[idx + 2]) if idx + 2 < n_units else None
            mx8 = None
            pts = []
            for r in range(SOFTMAX_ROW_CHUNKS):
                if idx + 1 < n_units:
                    part = fold8(rows(st_nxt, r), jnp.max)
                    mx8 = part if mx8 is None else jnp.maximum(mx8, part)
                pts.append(jnp.exp2(rows(st_cur, r) - m_new).astype(BF16))
            pt = jnp.concatenate(pts, axis=0)
            mloc_nxt = None if mx8 is None else jnp.max(mx8, axis=0, keepdims=True)
            vt = jnp.concatenate([vt_ref[hd * V_HEAD:(hd + 1) * V_HEAD, k0:k0 + kw],
                                  jnp.ones((BF16_ROWS, kw), BF16)], axis=0)
            pv = _dot(vt, pt)
            l_ref[hd, :, cols] = alpha * l_ref[hd, :, cols] + pv[V_HEAD:V_HEAD + 1]
            acc_ref[hd, :, cols] = alpha * acc_ref[hd, :, cols] + pv[:V_HEAD]
            m_ref[hd, :, cols] = m_new
            st_cur, mloc_cur, st_nxt = st_nxt, mloc_nxt, st_nn

    @pl.when(kind == 0)
    def _():
        step(None)

    for d in range(blk_k // blk_q):
        @pl.when(kind == 1 + d)
        def _():
            step(d * blk_q)
            at = jnp.concatenate([acc_ref[hd] * (1.0 / l_ref[hd]) for hd in range(N_HEADS)], axis=0)
            a_ref[...] = jnp.transpose(at)


def _prompt_attend(k_pk, q_t, v_t, *, batch, seq, blk_q, blk_k):
    assert blk_k % blk_q == 0 and seq % blk_k == 0
    nq, nk = seq // blk_q, seq // blk_k
    pairs = []
    for q in range(nq):
        k_last = (q * blk_q) // blk_k
        pairs += [(q, k, 0) for k in range(k_last)]
        pairs.append((q, k_last, 1 + (q * blk_q - k_last * blk_k) // blk_q))
    qi, ki, kind = (jnp.asarray([p[i] for p in pairs], jnp.int32) for i in range(3))
    n_tok = batch * seq
    grid_spec = pltpu.PrefetchScalarGridSpec(
        num_scalar_prefetch=3,
        grid=(batch, len(pairs)),
        in_specs=[
            pl.BlockSpec((N_HEADS, blk_k, LANES), lambda b, p, qi, ki, kd: (0, b * nk + ki[p], 0)),
            pl.BlockSpec((None, N_HEADS, LANES, blk_q), lambda b, p, qi, ki, kd: (b, 0, 0, qi[p])),
            pl.BlockSpec((None, MLA_WIDTH, blk_k), lambda b, p, qi, ki, kd: (b, 0, ki[p])),
        ],
        out_specs=pl.BlockSpec((blk_q, MLA_WIDTH), lambda b, p, qi, ki, kd: (b * nq + qi[p], 0)),
        scratch_shapes=[pltpu.VMEM((N_HEADS, 1, blk_q), F32), pltpu.VMEM((N_HEADS, 1, blk_q), F32),
                        pltpu.VMEM((N_HEADS, V_HEAD, blk_q), F32)],
    )
    return pl.pallas_call(
        _prompt_attn_kernel,
        grid_spec=grid_spec,
        out_shape=jax.ShapeDtypeStruct((n_tok, MLA_WIDTH), F32),
        compiler_params=pltpu.CompilerParams(dimension_semantics=("parallel", "arbitrary"),
                                             vmem_limit_bytes=VMEM_LIMIT),
        name="prompt_attend",
    )(qi, ki, kind, k_pk, q_t, v_t)


N_SLOTS = 3
DECODE_SPLIT = 4


def _local_softmax(s, v_b):
    m = jnp.max(s, axis=-1, keepdims=True)
    p = jnp.exp2(s - m)
    return m, jnp.sum(p, axis=-1, keepdims=True), _dot(p.astype(BF16), v_b)


def _merge_softmax(parts, m_ref, l_ref, acc_ref):
    m_run = m_ref[...]
    m_new = m_run
    for m, _, _ in parts:
        m_new = jnp.maximum(m_new, m)
    w_run = jnp.exp2(m_run - m_new)
    l_new = w_run * l_ref[...]
    acc_new = w_run * acc_ref[...]
    for m, l, acc in parts:
        w = jnp.exp2(m - m_new)
        l_new = l_new + w * l
        acc_new = acc_new + w * acc
    m_ref[...] = m_new
    l_ref[...] = l_new
    acc_ref[...] = acc_new


def _decode_attn_kernel(n_pg, pt_ref, ql_ref, qr_ref, cn_ref, rn_ref, ckv_hbm, krt_hbm, o_ref,
                        kbuf, rbuf, sems, m_ref, l_ref, acc_ref):
    b = pl.program_id(0)
    j = pl.program_id(1)
    nj = pl.num_programs(1)
    chunk = b * nj + j
    n_chunks = pl.num_programs(0) * nj
    slot = chunk % N_SLOTS

    def page_copies(ck, sl, i):
        page = pt_ref[ck * n_pg + i]
        rows = pl.ds(i * PAGE_SIZE, PAGE_SIZE)
        return (pltpu.make_async_copy(ckv_hbm.at[0, page], kbuf.at[sl, rows], sems.at[0, sl]),
                pltpu.make_async_copy(krt_hbm.at[0, page], rbuf.at[sl, :, rows], sems.at[1, sl]))

    def start_pages(ck, sl, pages):
        for i in pages:
            for cp in page_copies(ck, sl, i):
                cp.start()

    def wait_chunk(sl):
        for i in range(n_pg):
            for cp in page_copies(0, sl, i):
                cp.wait()

    ahead = N_SLOTS - 1

    @pl.when(chunk == 0)
    def _():
        for ck in range(ahead):
            start_pages(ck, ck, range(n_pg))

    nxt = jnp.minimum(chunk + ahead, n_chunks - 1)
    nxt_slot = (chunk + ahead) % N_SLOTS

    @pl.when(j == 0)
    def _():
        m_ref[...] = jnp.full(m_ref.shape, -jnp.inf, F32)
        l_ref[...] = jnp.zeros(l_ref.shape, F32)
        acc_ref[...] = jnp.zeros(acc_ref.shape, F32)

    n_rows = N_HEADS * ql_ref.shape[1]
    ql = ql_ref[...].reshape(n_rows, KV_RANK).astype(BF16)
    qr = qr_ref[...].reshape(n_rows, QK_ROPE).astype(BF16)

    wait_chunk(slot)

    keys = n_pg * PAGE_SIZE // DECODE_SPLIT
    pages_per_part = n_pg // DECODE_SPLIT
    kbs, scores = [], []
    for c in range(DECODE_SPLIT):
        kb = kbuf[slot, c * keys:(c + 1) * keys, :].astype(BF16)
        rbt = rbuf[slot, :, c * keys:(c + 1) * keys].astype(BF16)
        kbs.append(kb)
        scores.append(_dot_nt(ql, kb) + _dot(qr, rbt))
        start_pages(nxt, nxt_slot, range(c * pages_per_part, (c + 1) * pages_per_part))
    parts = [_local_softmax(s, kb) for s, kb in zip(scores, kbs)]
    _merge_softmax(parts, m_ref, l_ref, acc_ref)

    @pl.when(chunk == n_chunks - 1)
    def _():
        for extra in range(1, ahead + 1):
            wait_chunk((chunk + extra) % N_SLOTS)

    @pl.when(j == nj - 1)
    def _():
        t_new = cn_ref.shape[0]
        cb = cn_ref[...].astype(BF16)
        s_new = _dot_nt(ql, cb) + _dot_nt(qr, rn_ref[...].astype(BF16))
        r_pos = lax.broadcasted_iota(jnp.int32, s_new.shape, 0) % t_new
        c_pos = lax.broadcasted_iota(jnp.int32, s_new.shape, 1)
        s_new = jnp.where(c_pos <= r_pos, s_new, -jnp.inf)
        _merge_softmax([_local_softmax(s_new, cb)], m_ref, l_ref, acc_ref)
        o = acc_ref[...] * (1.0 / l_ref[...])
        o_ref[...] = o.reshape(o_ref.shape)


def _decode_attend(q_lat, q_rope, ckv_new, kr_new, cache_ckv, cache_krope_t, page_table, *, t_new, n_pg):
    dec_batch, n_pages = page_table.shape
    n_tok = dec_batch * t_new
    assert n_pages % n_pg == 0 and dec_batch * (n_pages // n_pg) >= N_SLOTS
    pt_flat = page_table.reshape(-1)
    in_specs = [
        pl.BlockSpec((N_HEADS, t_new, KV_RANK), lambda b, j, pt: (0, b, 0)),
        pl.BlockSpec((N_HEADS, t_new, QK_ROPE), lambda b, j, pt: (0, b, 0)),
        pl.BlockSpec((t_new, KV_RANK), lambda b, j, pt: (b, 0)),
        pl.BlockSpec((t_new, QK_ROPE), lambda b, j, pt: (b, 0)),
        pl.BlockSpec(memory_space=pl.ANY),
        pl.BlockSpec(memory_space=pl.ANY),
    ]
    rows = N_HEADS * t_new
    grid_spec = pltpu.PrefetchScalarGridSpec(
        num_scalar_prefetch=1,
        grid=(dec_batch, n_pages // n_pg),
        in_specs=in_specs,
        out_specs=pl.BlockSpec((N_HEADS, t_new, KV_RANK), lambda b, j, pt: (0, b, 0)),
        scratch_shapes=[pltpu.VMEM((N_SLOTS, n_pg * PAGE_SIZE, KV_RANK), F32),
                        pltpu.VMEM((N_SLOTS, QK_ROPE, n_pg * PAGE_SIZE), F32),
                        pltpu.SemaphoreType.DMA((2, N_SLOTS)),
                        pltpu.VMEM((rows, 1), F32), pltpu.VMEM((rows, 1), F32),
                        pltpu.VMEM((rows, KV_RANK), F32)],
    )
    return pl.pallas_call(
        functools.partial(_decode_attn_kernel, n_pg),
        grid_spec=grid_spec,
        out_shape=jax.ShapeDtypeStruct((N_HEADS, n_tok, KV_RANK), F32),
        compiler_params=pltpu.CompilerParams(dimension_semantics=("arbitrary", "arbitrary"),
                                             vmem_limit_bytes=VMEM_LIMIT),
        name="decode_attend",
    )(pt_flat, q_lat, q_rope, ckv_new, kr_new, cache_ckv, cache_krope_t)


def _back_kernel(latent_attn, *refs):
    if latent_attn:
        (o_ref, u_ref, v_ref, x_ref, wuv_ref, wmix_ref, bmix_ref, ga_ref, gg_ref, wout_ref,
         g2_ref, wgate_ref, wup_ref, wdown_ref, gf_ref, y_ref) = refs
        o_flat = jnp.concatenate([o_ref[hd].astype(BF16) for hd in range(N_HEADS)], axis=1)
        a = _dot(o_flat, wuv_ref[...])
    else:
        (a_ref, u_ref, v_ref, x_ref, wmix_ref, bmix_ref, ga_ref, gg_ref, wout_ref,
         g2_ref, wgate_ref, wup_ref, wdown_ref, gf_ref, y_ref) = refs
        a = a_ref[...]
    tm = x_ref.shape[0]
    an = _rms(a, ga_ref[...])

    lane = lax.broadcasted_iota(jnp.int32, (CHUNK, GMLP_WIDTH), 1)
    wmix = wmix_ref[...]
    mixes = []
    for c in range(tm // CHUNK):
        vck = v_ref[c * CHUNK:(c + 1) * CHUNK, :].astype(BF16)
        bd = jnp.concatenate(
            [jnp.where((lane >= g * GROUP_DIM) & (lane < (g + 1) * GROUP_DIM), vck, jnp.zeros_like(vck))
             for g in range(N_GROUPS)], axis=0)
        mixes.append(_dot(wmix, bd) + bmix_ref[...])
    mix = mixes[0] if len(mixes) == 1 else jnp.concatenate(mixes, axis=0)
    gn = _rms(u_ref[...] * mix, gg_ref[...])

    merged = jnp.concatenate([an, gn], axis=1).astype(BF16)
    x1 = x_ref[...] + _dot(merged, wout_ref[...])
    h2 = _rms(x1, g2_ref[...]).astype(BF16)
    gate = _dot(h2, wgate_ref[...])
    up = _dot(h2, wup_ref[...])
    act = (gate * (1.0 / (1.0 + jnp.exp(-gate))) * up).astype(BF16)
    x2 = x1 + _dot(act, wdown_ref[...])
    y_ref[...] = _rms(x2, gf_ref[...])


def _back(attn, u, v, x, wts, *, tm, name):
    n_tok, d_model = x.shape
    latent_attn = attn.ndim == 3
    row = lambda w: pl.BlockSpec((tm, w), lambda i: (i, 0))
    attn_spec = pl.BlockSpec((N_HEADS, tm, KV_RANK), lambda i: (0, i, 0)) if latent_attn else row(MLA_WIDTH)
    in_specs = [attn_spec, row(GMLP_WIDTH), row(GMLP_WIDTH), row(d_model)]
    in_specs += [pl.BlockSpec(w.shape, lambda i, _nd=w.ndim: (0,) * _nd, pipeline_mode=pl.Buffered(1))
                 for w in wts]
    return pl.pallas_call(
        functools.partial(_back_kernel, latent_attn),
        grid=(n_tok // tm,),
        in_specs=in_specs,
        out_specs=row(d_model),
        out_shape=jax.ShapeDtypeStruct((n_tok, d_model), F32),
        compiler_params=pltpu.CompilerParams(dimension_semantics=("parallel",),
                                             vmem_limit_bytes=VMEM_LIMIT),
        name=name,
    )(attn, u, v, x, *wts)


def _rope_angles(pos):
    inv = ROPE_BASE ** (-jnp.arange(ROPE_HALF, dtype=F32) / ROPE_HALF)
    ang = pos.astype(F32)[:, None] * inv[None, :]
    return jnp.cos(ang), jnp.sin(ang)


def _lane_tiled(t, reps_rows):
    return jnp.tile(t, (reps_rows, LANES // ROPE_HALF))


def kernel(x_prompt, x_sample, cache_ckv, cache_krope, page_table, norm1_g, w_in, q_norm_g, w_q_up, kv_norm_g, w_uk, w_uv, v_norm_g, v_norm_b, w_spatial, b_spatial, out_norm_mla_g, out_norm_gmlp_g, w_out, norm2_g, w_gate, w_up, w_down, final_norm_g):
    batch, seq, d_model = x_prompt.shape
    dec_batch, t_new, _ = x_sample.shape
    depth = w_in.shape[0]
    past_len = page_table.shape[1] * PAGE_SIZE
    assert depth == 1 and seq % CHUNK == 0 and t_new <= CHUNK and CHUNK % t_new == 0

    xp = x_prompt.reshape(batch * seq, d_model)
    xs = x_sample.reshape(dec_batch * t_new, d_model)
    cos_p, sin_p = _rope_angles(jnp.arange(seq))
    cos_s, sin_s = _rope_angles(past_len + jnp.arange(t_new))
    tm_front, tm_tok, n_pg = 512, 256, page_table.shape[1]
    attn_blk_q, attn_blk_k = 1024, 1024
    assert tm_tok % t_new == 0 and seq % tm_front == 0
    tables_p = (_lane_tiled(cos_p, 1), _lane_tiled(sin_p, 1), cos_p.T, sin_p.T)
    tables_s = (_lane_tiled(cos_s, tm_tok // t_new), _lane_tiled(sin_s, tm_tok // t_new))

    l = 0
    row2 = lambda g: g.reshape(1, -1)
    wi = w_in[l]
    off_kr = Q_RANK + KV_RANK
    off_g = off_kr + QK_ROPE
    w_in_p = jnp.concatenate(
        [wi[:, :off_kr], wi[:, off_g:], jnp.zeros((d_model, PK_R1), wi.dtype), wi[:, off_kr:off_g],
         jnp.zeros((d_model, LANES - PK_END), wi.dtype)], axis=1).astype(BF16)
    wq = w_q_up[l].reshape(Q_RANK, N_HEADS, QK_NOPE + QK_ROPE)
    wq_p = jnp.concatenate([wq, jnp.zeros((Q_RANK, N_HEADS, LANES - PK_END), wq.dtype)], axis=2)
    wq_p = wq_p.reshape(Q_RANK, N_HEADS * LANES).astype(BF16)
    wuk = jnp.transpose(w_uk[l], (1, 2, 0))
    wuk_h = jnp.concatenate([wuk, jnp.zeros((N_HEADS, LANES - QK_NOPE, KV_RANK), wuk.dtype)],
                            axis=1).astype(BF16)
    wuk_cols = jnp.transpose(wuk_h, (2, 0, 1)).reshape(KV_RANK, N_HEADS * LANES)
    front_tail = (row2(v_norm_g[l]), row2(v_norm_b[l]))

    eye_h = jnp.eye(N_HEADS, dtype=w_uv.dtype)
    wuv_bd = jnp.einsum('rhv,hg->hrgv', w_uv[l], eye_h).reshape(N_HEADS * KV_RANK, MLA_WIDTH).astype(BF16)
    tril = jnp.tril(jnp.ones((CHUNK, CHUNK), dtype=bool))
    wmix_p = jnp.where(tril, w_spatial[l], 0)
    bmix_p = jnp.repeat(b_spatial[l].T, GROUP_DIM, axis=1)
    reps = CHUNK // t_new
    tril_s = jnp.tril(jnp.ones((t_new, t_new), dtype=bool))
    w_small = jnp.where(tril_s, w_spatial[l][:, :t_new, :t_new], 0)
    wmix_s = jnp.einsum('ab,gij->gaibj', jnp.eye(reps, dtype=w_small.dtype), w_small)
    wmix_s = wmix_s.reshape(N_GROUPS, CHUNK, CHUNK)
    bmix_s = jnp.tile(jnp.repeat(b_spatial[l][:, :t_new].T, GROUP_DIM, axis=1), (reps, 1))
    cat_groups = lambda w: jnp.transpose(w, (1, 0, 2)).reshape(CHUNK, N_GROUPS * CHUNK).astype(BF16)
    back_tail = (row2(out_norm_mla_g[l]), row2(out_norm_gmlp_g[l]), w_out[l].astype(BF16), row2(norm2_g[l]),
                 w_gate[l].astype(BF16), w_up[l].astype(BF16), w_down[l].astype(BF16), row2(final_norm_g))

    wuv_t = w_uv[l].reshape(KV_RANK, MLA_WIDTH).T.astype(BF16)
    wts_p = (row2(norm1_g[l]), w_in_p, row2(q_norm_g[l]), wq_p.T, row2(kv_norm_g[l]), wuk_cols, wuv_t) + front_tail
    q_t, k_pk, ckv_p, v_t, kr_p, u_p, v_p = _front(xp, tables_p, wts_p, absorbed=False, tm=tm_front, seq=seq)
    a_p = _prompt_attend(k_pk, q_t, v_t, batch=batch, seq=seq, blk_q=attn_blk_q, blk_k=attn_blk_k)
    y_p = _back(a_p, u_p, v_p, xp, (cat_groups(wmix_p), bmix_p) + back_tail, tm=tm_tok, name="back_prompt")

    wts_s = (row2(norm1_g[l]), w_in_p, row2(q_norm_g[l]), wq_p, row2(kv_norm_g[l]), wuk_h) + front_tail
    q_lat, q_rope, ckv_s, kr_s, u_s, v_s = _front(xs, tables_s, wts_s, absorbed=True, tm=tm_tok)
    o_s = _decode_attend(q_lat, q_rope, ckv_s, kr_s, cache_ckv, jnp.swapaxes(cache_krope, 2, 3), page_table,
                         t_new=t_new, n_pg=n_pg)
    y_s = _back(o_s, u_s, v_s, xs, (wuv_bd, cat_groups(wmix_s), bmix_s) + back_tail, tm=tm_tok, name="back_decode")

    return (y_p.reshape(batch, seq, d_model),
            y_s.reshape(dec_batch, t_new, d_model),
            ckv_p.reshape(depth, batch, seq, KV_RANK),
            kr_p.reshape(depth, batch, seq, QK_ROPE),
            ckv_s.reshape(depth, dec_batch, t_new, KV_RANK),
            kr_s.reshape(depth, dec_batch, t_new, QK_ROPE),
            v_s.reshape(depth, dec_batch, t_new, GMLP_WIDTH))
```

```python
import functools
import math

import jax
import jax.numpy as jnp
from jax import lax
from jax.experimental import pallas as pl
from jax.experimental.pallas import tpu as pltpu

LANES = 128
SUBLANES = 8
BF16_ROWS = 16
MXU_DIM = 256
VMEM_LIMIT = 56 * 1024 * 1024

N_HEADS = 8
QK_NOPE = 64
QK_ROPE = 32
ROPE_HALF = QK_ROPE // 2
Q_RANK = 384
KV_RANK = 256
V_HEAD = 64
N_GROUPS = 8
GROUP_DIM = 64
GMLP_WIDTH = N_GROUPS * GROUP_DIM
MLA_WIDTH = N_HEADS * V_HEAD
CHUNK = 128
PAGE_SIZE = 128
ROPE_BASE = 10000.0
EPS = 1e-6
Q_SCALE = (QK_NOPE + QK_ROPE) ** -0.5 * math.log2(math.e)

COL_Q = 0
COL_KV = COL_Q + Q_RANK
COL_U = COL_KV + KV_RANK
COL_V = COL_U + GMLP_WIDTH
COL_KR = COL_V + GMLP_WIDTH
IN_COLS_PAD = COL_KR + LANES
PK_R1 = QK_NOPE
PK_R2 = QK_NOPE + ROPE_HALF
PK_END = QK_NOPE + QK_ROPE

BF16 = jnp.bfloat16
F32 = jnp.float32


def _rms(x, g):
    return x * lax.rsqrt(jnp.mean(x * x, axis=-1, keepdims=True) + EPS) * g


def _dot(a, b):
    return jnp.dot(a, b, preferred_element_type=F32)


def _dot_nt(a, b):
    return lax.dot_general(a, b, (((1,), (1,)), ((), ())), preferred_element_type=F32)


def _rope_packed(t, c_mul, s_up, s_down):
    return (t * c_mul
            + pltpu.roll(t, LANES - ROPE_HALF, axis=1) * s_up
            + pltpu.roll(t, ROPE_HALF, axis=1) * s_down)


def _front_kernel(absorbed, *refs):
    if absorbed:
        (x_ref, cos_ref, sin_ref, g1_ref, w_in_ref, gq_ref, wq_ref, gkv_ref, wuk_ref, vg_ref, vb_ref,
         qlat_ref, qrope_ref, ckv_ref, kr_ref, u_ref, v_ref) = refs
    else:
        (x_ref, cos_ref, sin_ref, cost_ref, sint_ref, g1_ref, w_in_ref, gq_ref, wq_ref, gkv_ref, wuk_ref,
         wuvt_ref, vg_ref, vb_ref, qt_ref, kpk_ref, ckv_ref, vt_ref, kr_ref, u_ref, v_ref) = refs
    x = x_ref[...]
    h = _rms(x, g1_ref[...]).astype(BF16)
    z = _dot(h, w_in_ref[...])

    cos_t = cos_ref[...]
    sin_t = sin_ref[...]
    lane = lax.broadcasted_iota(jnp.int32, cos_t.shape, 1)
    in_r1 = (lane >= PK_R1) & (lane < PK_R2)
    in_r2 = (lane >= PK_R2) & (lane < PK_END)

    ckv = _rms(z[:, COL_KV:COL_U], gkv_ref[...])
    ckv_ref[...] = ckv
    k_c = jnp.where(in_r1 | in_r2, cos_t, 0.0)
    k_up = jnp.where(in_r1, -sin_t, 0.0)
    k_dn = jnp.where(in_r2, sin_t, 0.0)
    kr = _rope_packed(z[:, COL_KR:COL_KR + LANES], k_c, k_up, k_dn)
    if absorbed:
        kr_ref[...] = kr[:, PK_R1:PK_END]
    else:
        kr_ref[...] = jnp.transpose(kr)[PK_R1:PK_END]

    cqn = _rms(z[:, COL_Q:COL_KV], gq_ref[...]).astype(BF16)
    if absorbed:
        qp = _dot(cqn, wq_ref[...])
        q_c = jnp.where(lane < PK_R1, Q_SCALE, jnp.where(lane < PK_END, Q_SCALE * cos_t, 0.0))
        q_up = jnp.where(in_r1, -Q_SCALE * sin_t, 0.0)
        q_dn = jnp.where(in_r2, Q_SCALE * sin_t, 0.0)
        for hd in range(N_HEADS):
            qh = _rope_packed(qp[:, hd * LANES:(hd + 1) * LANES], q_c, q_up, q_dn)
            qlat_ref[hd] = _dot(qh.astype(BF16), wuk_ref[hd])
            qrope_ref[hd] = qh[:, PK_R1:PK_END]
    else:
        qpt = _dot_nt(wq_ref[...], cqn)
        c_t = cost_ref[...] * Q_SCALE
        s_t = sint_ref[...] * Q_SCALE
        zpad = jnp.zeros((LANES - PK_END, qpt.shape[1]), F32)
        for hd in range(N_HEADS):
            blk = qpt[hd * LANES:(hd + 1) * LANES]
            x1 = blk[PK_R1:PK_R2]
            x2 = blk[PK_R2:PK_END]
            qt = jnp.concatenate([blk[:PK_R1] * Q_SCALE, x1 * c_t - x2 * s_t, x1 * s_t + x2 * c_t, zpad],
                                 axis=0)
            qt_ref[hd] = qt.astype(BF16)
        ckv_b = ckv.astype(BF16)
        knope = _dot(ckv_b, wuk_ref[...])
        for hd in range(N_HEADS):
            kpk_ref[hd] = (knope[:, hd * LANES:(hd + 1) * LANES] + kr).astype(BF16)
        vt_ref[...] = _dot_nt(wuvt_ref[...], ckv_b).astype(BF16)

    zg = z[:, COL_U:COL_KR]
    zg = 0.5 * zg * (1.0 + lax.erf(zg * math.sqrt(0.5)))
    u_ref[...] = zg[:, :GMLP_WIDTH]
    vv = zg[:, GMLP_WIDTH:]
    mu = jnp.mean(vv, axis=-1, keepdims=True)
    vc = vv - mu
    v_ref[...] = vc * lax.rsqrt(jnp.mean(vc * vc, axis=-1, keepdims=True) + EPS) * vg_ref[...] + vb_ref[...]


def _const_spec(shape):
    nd = len(shape)
    return pl.BlockSpec(shape, lambda i, _nd=nd: (0,) * _nd)


def _front(x, tables, wts, *, absorbed, tm, seq=None):
    n_tok, d_model = x.shape
    grid = (n_tok // tm,)
    row = lambda w: pl.BlockSpec((tm, w), lambda i: (i, 0))
    head = lambda w, dt: (jax.ShapeDtypeStruct((N_HEADS, n_tok, w), dt),
                          pl.BlockSpec((N_HEADS, tm, w), lambda i: (0, i, 0)))
    flat = lambda w, dt: (jax.ShapeDtypeStruct((n_tok, w), dt), row(w))
    tab_blocks = tables[0].shape[0] // tm
    table_specs = [pl.BlockSpec((tm, LANES), lambda i: (i % tab_blocks, 0))] * 2
    if absorbed:
        outs = [head(KV_RANK, F32), head(QK_ROPE, F32), flat(KV_RANK, F32), flat(QK_ROPE, F32),
                flat(GMLP_WIDTH, F32), flat(GMLP_WIDTH, F32)]
    else:
        spb = seq // tm
        nb = n_tok // seq
        table_specs += [pl.BlockSpec((ROPE_HALF, tm), lambda i: (0, i % spb))] * 2
        qt = (jax.ShapeDtypeStruct((nb, N_HEADS, LANES, seq), BF16),
              pl.BlockSpec((None, N_HEADS, LANES, tm), lambda i: (i // spb, 0, 0, i % spb)))
        vt = (jax.ShapeDtypeStruct((nb, MLA_WIDTH, seq), BF16),
              pl.BlockSpec((None, MLA_WIDTH, tm), lambda i: (i // spb, 0, i % spb)))
        krt = (jax.ShapeDtypeStruct((nb, QK_ROPE, seq), F32),
               pl.BlockSpec((None, QK_ROPE, tm), lambda i: (i // spb, 0, i % spb)))
        outs = [qt, head(LANES, BF16), flat(KV_RANK, F32), vt, krt,
                flat(GMLP_WIDTH, F32), flat(GMLP_WIDTH, F32)]
    ins = [x] + list(tables) + list(wts)
    in_specs = [row(d_model)] + table_specs + [_const_spec(w.shape) for w in wts]
    return pl.pallas_call(
        functools.partial(_front_kernel, absorbed),
        grid=grid,
        in_specs=in_specs,
        out_specs=[o[1] for o in outs],
        out_shape=[o[0] for o in outs],
        compiler_params=pltpu.CompilerParams(dimension_semantics=("parallel",),
                                             vmem_limit_bytes=VMEM_LIMIT),
        name="front_absorbed" if absorbed else "front_prompt",
    )(*ins)


ATTN_UNIT_COLS = 2 * MXU_DIM
ATTN_UNIT_KEYS = 256
SOFTMAX_ROW_CHUNKS = 4
def _prompt_attn_kernel(qi_ref, ki_ref, kind_ref, k_ref, qt_ref, vt_ref, a_ref, m_ref, l_ref, acc_ref):
    p_id = pl.program_id(1)
    ki = ki_ref[p_id]
    kind = kind_ref[p_id]
    blk_k = k_ref.shape[1]
    blk_q = qt_ref.shape[2]
    qw = ATTN_UNIT_COLS

    @pl.when(ki == 0)
    def _():
        m_ref[...] = jnp.full(m_ref.shape, -jnp.inf, F32)
        l_ref[...] = jnp.zeros(l_ref.shape, F32)
        acc_ref[...] = jnp.zeros(acc_ref.shape, F32)

    def step(delta):
        kw = min(ATTN_UNIT_KEYS, blk_k)
        first_q = lambda c: delta + c * qw
        units = [(k0, c, hd) for k0 in range(0, blk_k, kw) for c in range(blk_q // qw)
                 if delta is None or k0 < first_q(c) + qw for hd in range(N_HEADS)]

        def scores(k0, c, hd):
            st = _dot(k_ref[hd, k0:k0 + kw, :], qt_ref[hd, :, c * qw:(c + 1) * qw])
            if delta is not None and k0 + kw - 1 > first_q(c):
                kpos = lax.broadcasted_iota(jnp.int32, (kw, qw), 0) + k0
                qpos = lax.broadcasted_iota(jnp.int32, (kw, qw), 1) + first_q(c)
                st = jnp.where(kpos <= qpos, st, -jnp.inf)
            return st

        def rows(st, r):
            rk = st.shape[0] // SOFTMAX_ROW_CHUNKS
            return st[r * rk:(r + 1) * rk]

        def fold8(x, op):
            return op(x.reshape(x.shape[0] // SUBLANES, SUBLANES, x.shape[1]), axis=0)

        n_units = len(units)
        st_cur = scores(*units[0])
        mloc_cur = jnp.max(st_cur, axis=0, keepdims=True)
        st_nxt = scores(*units[1])
        for idx, (k0, c, hd) in enumerate(units):
            cols = slice(c * qw, (c + 1) * qw)
            m_prev = m_ref[hd, :, cols]
            m_new = jnp.maximum(m_prev, mloc_cur)
            alpha = jnp.exp2(m_prev - m_new)
            st_nn = scores(*units[idx + 2]) if idx + 2 < n_units else None
            mx8 = None
            pts = []
            for r in range(SOFTMAX_ROW_CHUNKS):
                if idx + 1 < n_units:
                    part = fold8(rows(st_nxt, r), jnp.max)
                    mx8 = part if mx8 is None else jnp.maximum(mx8, part)
                pts.append(jnp.exp2(rows(st_cur, r) - m_new).astype(BF16))
            pt = jnp.concatenate(pts, axis=0)
            mloc_nxt = None if mx8 is None else jnp.max(mx8, axis=0, keepdims=True)
            vt = jnp.concatenate([vt_ref[hd * V_HEAD:(hd + 1) * V_HEAD, k0:k0 + kw],
                                  jnp.ones((BF16_ROWS, kw), BF16)], axis=0)
            pv = _dot(vt, pt)
            l_ref[hd, :, cols] = alpha * l_ref[hd, :, cols] + pv[V_HEAD:V_HEAD + 1]
            acc_ref[hd, :, cols] = alpha * acc_ref[hd, :, cols] + pv[:V_HEAD]
            m_ref[hd, :, cols] = m_new
            st_cur, mloc_cur, st_nxt = st_nxt, mloc_nxt, st_nn

    @pl.when(kind == 0)
    def _():
        step(None)

    for d in range(blk_k // blk_q):
        @pl.when(kind == 1 + d)
        def _():
            step(d * blk_q)
            at = jnp.concatenate([acc_ref[hd] * (1.0 / l_ref[hd]) for hd in range(N_HEADS)], axis=0)
            a_ref[...] = jnp.transpose(at)


def _prompt_attend(k_pk, q_t, v_t, *, batch, seq, blk_q, blk_k):
    assert blk_k % blk_q == 0 and seq % blk_k == 0
    nq, nk = seq // blk_q, seq // blk_k
    pairs = []
    for q in range(nq):
        k_last = (q * blk_q) // blk_k
        pairs += [(q, k, 0) for k in range(k_last)]
        pairs.append((q, k_last, 1 + (q * blk_q - k_last * blk_k) // blk_q))
    qi, ki, kind = (jnp.asarray([p[i] for p in pairs], jnp.int32) for i in range(3))
    n_tok = batch * seq
    grid_spec = pltpu.PrefetchScalarGridSpec(
        num_scalar_prefetch=3,
        grid=(batch, len(pairs)),
        in_specs=[
            pl.BlockSpec((N_HEADS, blk_k, LANES), lambda b, p, qi, ki, kd: (0, b * nk + ki[p], 0)),
            pl.BlockSpec((None, N_HEADS, LANES, blk_q), lambda b, p, qi, ki, kd: (b, 0, 0, qi[p])),
            pl.BlockSpec((None, MLA_WIDTH, blk_k), lambda b, p, qi, ki, kd: (b, 0, ki[p])),
        ],
        out_specs=pl.BlockSpec((blk_q, MLA_WIDTH), lambda b, p, qi, ki, kd: (b * nq + qi[p], 0)),
        scratch_shapes=[pltpu.VMEM((N_HEADS, 1, blk_q), F32), pltpu.VMEM((N_HEADS, 1, blk_q), F32),
                        pltpu.VMEM((N_HEADS, V_HEAD, blk_q), F32)],
    )
    return pl.pallas_call(
        _prompt_attn_kernel,
        grid_spec=grid_spec,
        out_shape=jax.ShapeDtypeStruct((n_tok, MLA_WIDTH), F32),
        compiler_params=pltpu.CompilerParams(dimension_semantics=("parallel", "arbitrary"),
                                             vmem_limit_bytes=VMEM_LIMIT),
        name="prompt_attend",
    )(qi, ki, kind, k_pk, q_t, v_t)


N_SLOTS = 3
DECODE_SPLIT = 4


def _local_softmax(s, v_b):
    m = jnp.max(s, axis=-1, keepdims=True)
    p = jnp.exp2(s - m)
    return m, jnp.sum(p, axis=-1, keepdims=True), _dot(p.astype(BF16), v_b)


def _merge_softmax(parts, m_ref, l_ref, acc_ref):
    m_run = m_ref[...]
    m_new = m_run
    for m, _, _ in parts:
        m_new = jnp.maximum(m_new, m)
    w_run = jnp.exp2(m_run - m_new)
    l_new = w_run * l_ref[...]
    acc_new = w_run * acc_ref[...]
    for m, l, acc in parts:
        w = jnp.exp2(m - m_new)
        l_new = l_new + w * l
        acc_new = acc_new + w * acc
    m_ref[...] = m_new
    l_ref[...] = l_new
    acc_ref[...] = acc_new


def _decode_attn_kernel(n_pg, pt_ref, ql_ref, qr_ref, cn_ref, rn_ref, ckv_hbm, krt_hbm, o_ref,
                        kbuf, rbuf, sems, m_ref, l_ref, acc_ref):
    b = pl.program_id(0)
    j = pl.program_id(1)
    nj = pl.num_programs(1)
    chunk = b * nj + j
    n_chunks = pl.num_programs(0) * nj
    slot = chunk % N_SLOTS

    def page_copies(ck, sl, i):
        page = pt_ref[ck * n_pg + i]
        rows = pl.ds(i * PAGE_SIZE, PAGE_SIZE)
        return (pltpu.make_async_copy(ckv_hbm.at[0, page], kbuf.at[sl, rows], sems.at[0, sl]),
                pltpu.make_async_copy(krt_hbm.at[0, page], rbuf.at[sl, :, rows], sems.at[1, sl]))

    def start_pages(ck, sl, pages):
        for i in pages:
            for cp in page_copies(ck, sl, i):
                cp.start()

    def wait_chunk(sl):
        for i in range(n_pg):
            for cp in page_copies(0, sl, i):
                cp.wait()

    ahead = N_SLOTS - 1

    @pl.when(chunk == 0)
    def _():
        for ck in range(ahead):
            start_pages(ck, ck, range(n_pg))

    nxt = jnp.minimum(chunk + ahead, n_chunks - 1)
    nxt_slot = (chunk + ahead) % N_SLOTS

    @pl.when(j == 0)
    def _():
        m_ref[...] = jnp.full(m_ref.shape, -jnp.inf, F32)
        l_ref[...] = jnp.zeros(l_ref.shape, F32)
        acc_ref[...] = jnp.zeros(acc_ref.shape, F32)

    n_rows = N_HEADS * ql_ref.shape[1]
    ql = ql_ref[...].reshape(n_rows, KV_RANK).astype(BF16)
    qr = qr_ref[...].reshape(n_rows, QK_ROPE).astype(BF16)

    wait_chunk(slot)

    keys = n_pg * PAGE_SIZE // DECODE_SPLIT
    pages_per_part = n_pg // DECODE_SPLIT
    kbs, scores = [], []
    for c in range(DECODE_SPLIT):
        kb = kbuf[slot, c * keys:(c + 1) * keys, :].astype(BF16)
        rbt = rbuf[slot, :, c * keys:(c + 1) * keys].astype(BF16)
        kbs.append(kb)
        scores.append(_dot_nt(ql, kb) + _dot(qr, rbt))
        start_pages(nxt, nxt_slot, range(c * pages_per_part, (c + 1) * pages_per_part))
    parts = [_local_softmax(s, kb) for s, kb in zip(scores, kbs)]
    _merge_softmax(parts, m_ref, l_ref, acc_ref)

    @pl.when(chunk == n_chunks - 1)
    def _():
        for extra in range(1, ahead + 1):
            wait_chunk((chunk + extra) % N_SLOTS)

    @pl.when(j == nj - 1)
    def _():
        t_new = cn_ref.shape[0]
        cb = cn_ref[...].astype(BF16)
        s_new = _dot_nt(ql, cb) + _dot_nt(qr, rn_ref[...].astype(BF16))
        r_pos = lax.broadcasted_iota(jnp.int32, s_new.shape, 0) % t_new
        c_pos = lax.broadcasted_iota(jnp.int32, s_new.shape, 1)
        s_new = jnp.where(c_pos <= r_pos, s_new, -jnp.inf)
        _merge_softmax([_local_softmax(s_new, cb)], m_ref, l_ref, acc_ref)
        o = acc_ref[...] * (1.0 / l_ref[...])
        o_ref[...] = o.reshape(o_ref.shape)


def _decode_attend(q_lat, q_rope, ckv_new, kr_new, cache_ckv, cache_krope_t, page_table, *, t_new, n_pg):
    dec_batch, n_pages = page_table.shape
    n_tok = dec_batch * t_new
    assert n_pages % n_pg == 0 and dec_batch * (n_pages // n_pg) >= N_SLOTS
    pt_flat = page_table.reshape(-1)
    in_specs = [
        pl.BlockSpec((N_HEADS, t_new, KV_RANK), lambda b, j, pt: (0, b, 0)),
        pl.BlockSpec((N_HEADS, t_new, QK_ROPE), lambda b, j, pt: (0, b, 0)),
        pl.BlockSpec((t_new, KV_RANK), lambda b, j, pt: (b, 0)),
        pl.BlockSpec((t_new, QK_ROPE), lambda b, j, pt: (b, 0)),
        pl.BlockSpec(memory_space=pl.ANY),
        pl.BlockSpec(memory_space=pl.ANY),
    ]
    rows = N_HEADS * t_new
    grid_spec = pltpu.PrefetchScalarGridSpec(
        num_scalar_prefetch=1,
        grid=(dec_batch, n_pages // n_pg),
        in_specs=in_specs,
        out_specs=pl.BlockSpec((N_HEADS, t_new, KV_RANK), lambda b, j, pt: (0, b, 0)),
        scratch_shapes=[pltpu.VMEM((N_SLOTS, n_pg * PAGE_SIZE, KV_RANK), F32),
                        pltpu.VMEM((N_SLOTS, QK_ROPE, n_pg * PAGE_SIZE), F32),
                        pltpu.SemaphoreType.DMA((2, N_SLOTS)),
                        pltpu.VMEM((rows, 1), F32), pltpu.VMEM((rows, 1), F32),
                        pltpu.VMEM((rows, KV_RANK), F32)],
    )
    return pl.pallas_call(
        functools.partial(_decode_attn_kernel, n_pg),
        grid_spec=grid_spec,
        out_shape=jax.ShapeDtypeStruct((N_HEADS, n_tok, KV_RANK), F32),
        compiler_params=pltpu.CompilerParams(dimension_semantics=("arbitrary", "arbitrary"),
                                             vmem_limit_bytes=VMEM_LIMIT),
        name="decode_attend",
    )(pt_flat, q_lat, q_rope, ckv_new, kr_new, cache_ckv, cache_krope_t)


def _back_kernel(latent_attn, *refs):
    if latent_attn:
        (o_ref, u_ref, v_ref, x_ref, wuv_ref, wmix_ref, bmix_ref, ga_ref, gg_ref, wout_ref,
         g2_ref, wgate_ref, wup_ref, wdown_ref, gf_ref, y_ref) = refs
        o_flat = jnp.concatenate([o_ref[hd].astype(BF16) for hd in range(N_HEADS)], axis=1)
        a = _dot(o_flat, wuv_ref[...])
    else:
        (a_ref, u_ref, v_ref, x_ref, wmix_ref, bmix_ref, ga_ref, gg_ref, wout_ref,
         g2_ref, wgate_ref, wup_ref, wdown_ref, gf_ref, y_ref) = refs
        a = a_ref[...]
    tm = x_ref.shape[0]
    an = _rms(a, ga_ref[...])

    lane = lax.broadcasted_iota(jnp.int32, (CHUNK, GMLP_WIDTH), 1)
    wmix = wmix_ref[...]
    mixes = []
    for c in range(tm // CHUNK):
        vck = v_ref[c * CHUNK:(c + 1) * CHUNK, :].astype(BF16)
        bd = jnp.concatenate(
            [jnp.where((lane >= g * GROUP_DIM) & (lane < (g + 1) * GROUP_DIM), vck, jnp.zeros_like(vck))
             for g in range(N_GROUPS)], axis=0)
        mixes.append(_dot(wmix, bd) + bmix_ref[...])
    mix = mixes[0] if len(mixes) == 1 else jnp.concatenate(mixes, axis=0)
    gn = _rms(u_ref[...] * mix, gg_ref[...])

    merged = jnp.concatenate([an, gn], axis=1).astype(BF16)
    x1 = x_ref[...] + _dot(merged, wout_ref[...])
    h2 = _rms(x1, g2_ref[...]).astype(BF16)
    gate = _dot(h2, wgate_ref[...])
    up = _dot(h2, wup_ref[...])
    act = (gate * (1.0 / (1.0 + jnp.exp(-gate))) * up).astype(BF16)
    x2 = x1 + _dot(act, wdown_ref[...])
    y_ref[...] = _rms(x2, gf_ref[...])


def _back(attn, u, v, x, wts, *, tm, name):
    n_tok, d_model = x.shape
    latent_attn = attn.ndim == 3
    row = lambda w: pl.BlockSpec((tm, w), lambda i: (i, 0))
    attn_spec = pl.BlockSpec((N_HEADS, tm, KV_RANK), lambda i: (0, i, 0)) if latent_attn else row(MLA_WIDTH)
    in_specs = [attn_spec, row(GMLP_WIDTH), row(GMLP_WIDTH), row(d_model)]
    in_specs += [pl.BlockSpec(w.shape, lambda i, _nd=w.ndim: (0,) * _nd, pipeline_mode=pl.Buffered(1))
                 for w in wts]
    return pl.pallas_call(
        functools.partial(_back_kernel, latent_attn),
        grid=(n_tok // tm,),
        in_specs=in_specs,
        out_specs=row(d_model),
        out_shape=jax.ShapeDtypeStruct((n_tok, d_model), F32),
        compiler_params=pltpu.CompilerParams(dimension_semantics=("parallel",),
                                             vmem_limit_bytes=VMEM_LIMIT),
        name=name,
    )(attn, u, v, x, *wts)


def _rope_angles(pos):
    inv = ROPE_BASE ** (-jnp.arange(ROPE_HALF, dtype=F32) / ROPE_HALF)
    ang = pos.astype(F32)[:, None] * inv[None, :]
    return jnp.cos(ang), jnp.sin(ang)


def _lane_tiled(t, reps_rows):
    return jnp.tile(t, (reps_rows, LANES // ROPE_HALF))


def kernel(x_prompt, x_sample, cache_ckv, cache_krope, page_table, norm1_g, w_in, q_norm_g, w_q_up, kv_norm_g, w_uk, w_uv, v_norm_g, v_norm_b, w_spatial, b_spatial, out_norm_mla_g, out_norm_gmlp_g, w_out, norm2_g, w_gate, w_up, w_down, final_norm_g):
    batch, seq, d_model = x_prompt.shape
    dec_batch, t_new, _ = x_sample.shape
    depth = w_in.shape[0]
    past_len = page_table.shape[1] * PAGE_SIZE
    assert depth == 1 and seq % CHUNK == 0 and t_new <= CHUNK and CHUNK % t_new == 0

    xp = x_prompt.reshape(batch * seq, d_model)
    xs = x_sample.reshape(dec_batch * t_new, d_model)
    cos_p, sin_p = _rope_angles(jnp.arange(seq))
    cos_s, sin_s = _rope_angles(past_len + jnp.arange(t_new))
    tm_front, tm_tok, n_pg = 1024, 512, page_table.shape[1]
    attn_blk_q, attn_blk_k = 1024, 1024
    assert tm_tok % t_new == 0 and seq % tm_front == 0
    tables_p = (_lane_tiled(cos_p, 1), _lane_tiled(sin_p, 1), cos_p.T, sin_p.T)
    tables_s = (_lane_tiled(cos_s, tm_tok // t_new), _lane_tiled(sin_s, tm_tok // t_new))

    l = 0
    row2 = lambda g: g.reshape(1, -1)
    wi = w_in[l]
    off_kr = Q_RANK + KV_RANK
    off_g = off_kr + QK_ROPE
    w_in_p = jnp.concatenate(
        [wi[:, :off_kr], wi[:, off_g:], jnp.zeros((d_model, PK_R1), wi.dtype), wi[:, off_kr:off_g],
         jnp.zeros((d_model, LANES - PK_END), wi.dtype)], axis=1).astype(BF16)
    wq = w_q_up[l].reshape(Q_RANK, N_HEADS, QK_NOPE + QK_ROPE)
    wq_p = jnp.concatenate([wq, jnp.zeros((Q_RANK, N_HEADS, LANES - PK_END), wq.dtype)], axis=2)
    wq_p = wq_p.reshape(Q_RANK, N_HEADS * LANES).astype(BF16)
    wuk = jnp.transpose(w_uk[l], (1, 2, 0))
    wuk_h = jnp.concatenate([wuk, jnp.zeros((N_HEADS, LANES - QK_NOPE, KV_RANK), wuk.dtype)],
                            axis=1).astype(BF16)
    wuk_cols = jnp.transpose(wuk_h, (2, 0, 1)).reshape(KV_RANK, N_HEADS * LANES)
    front_tail = (row2(v_norm_g[l]), row2(v_norm_b[l]))

    eye_h = jnp.eye(N_HEADS, dtype=w_uv.dtype)
    wuv_bd = jnp.einsum('rhv,hg->hrgv', w_uv[l], eye_h).reshape(N_HEADS * KV_RANK, MLA_WIDTH).astype(BF16)
    tril = jnp.tril(jnp.ones((CHUNK, CHUNK), dtype=bool))
    wmix_p = jnp.where(tril, w_spatial[l], 0)
    bmix_p = jnp.repeat(b_spatial[l].T, GROUP_DIM, axis=1)
    reps = CHUNK // t_new
    tril_s = jnp.tril(jnp.ones((t_new, t_new), dtype=bool))
    w_small = jnp.where(tril_s, w_spatial[l][:, :t_new, :t_new], 0)
    pos = jnp.arange(CHUNK) // t_new
    same_elem = pos[:, None] == pos[None, :]
    wmix_s = jnp.where(same_elem, jnp.tile(w_small, (1, reps, reps)), 0)
    bmix_s = jnp.tile(jnp.repeat(b_spatial[l][:, :t_new].T, GROUP_DIM, axis=1), (reps, 1))
    cat_groups = lambda w: jnp.transpose(w, (1, 0, 2)).reshape(CHUNK, N_GROUPS * CHUNK).astype(BF16)
    back_tail = (row2(out_norm_mla_g[l]), row2(out_norm_gmlp_g[l]), w_out[l].astype(BF16), row2(norm2_g[l]),
                 w_gate[l].astype(BF16), w_up[l].astype(BF16), w_down[l].astype(BF16), row2(final_norm_g))

    wuv_t = w_uv[l].reshape(KV_RANK, MLA_WIDTH).T.astype(BF16)
    wts_p = (row2(norm1_g[l]), w_in_p, row2(q_norm_g[l]), wq_p.T, row2(kv_norm_g[l]), wuk_cols, wuv_t) + front_tail
    q_t, k_pk, ckv_p, v_t, kr_p, u_p, v_p = _front(xp, tables_p, wts_p, absorbed=False, tm=tm_front, seq=seq)
    a_p = _prompt_attend(k_pk, q_t, v_t, batch=batch, seq=seq, blk_q=attn_blk_q, blk_k=attn_blk_k)
    y_p = _back(a_p, u_p, v_p, xp, (cat_groups(wmix_p), bmix_p) + back_tail, tm=tm_tok, name="back_prompt")

    wts_s = (row2(norm1_g[l]), w_in_p, row2(q_norm_g[l]), wq_p, row2(kv_norm_g[l]), wuk_h) + front_tail
    q_lat, q_rope, ckv_s, kr_s, u_s, v_s = _front(xs, tables_s, wts_s, absorbed=True, tm=tm_tok)
    o_s = _decode_attend(q_lat, q_rope, ckv_s, kr_s, cache_ckv, jnp.swapaxes(cache_krope, 2, 3), page_table,
                         t_new=t_new, n_pg=n_pg)
    y_s = _back(o_s, u_s, v_s, xs, (wuv_bd, cat_groups(wmix_s), bmix_s) + back_tail, tm=tm_tok, name="back_decode")

    return (y_p.reshape(batch, seq, d_model),
            y_s.reshape(dec_batch, t_new, d_model),
            ckv_p.reshape(depth, batch, seq, KV_RANK),
            jnp.swapaxes(kr_p, 1, 2).reshape(depth, batch, seq, QK_ROPE),
            ckv_s.reshape(depth, dec_batch, t_new, KV_RANK),
            kr_s.reshape(depth, dec_batch, t_new, QK_ROPE),
            v_s.reshape(depth, dec_batch, t_new, GMLP_WIDTH))
```

```python
import functools
import math

import jax
import jax.numpy as jnp
from jax import lax
from jax.experimental import pallas as pl
from jax.experimental.pallas import tpu as pltpu

LANES = 128
SUBLANES = 8
BF16_ROWS = 16
MXU_DIM = 256
VMEM_LIMIT = 56 * 1024 * 1024

N_HEADS = 8
QK_NOPE = 64
QK_ROPE = 32
ROPE_HALF = QK_ROPE // 2
Q_RANK = 384
KV_RANK = 256
V_HEAD = 64
N_GROUPS = 8
GROUP_DIM = 64
GMLP_WIDTH = N_GROUPS * GROUP_DIM
MLA_WIDTH = N_HEADS * V_HEAD
CHUNK = 128
PAGE_SIZE = 128
ROPE_BASE = 10000.0
EPS = 1e-6
Q_SCALE = (QK_NOPE + QK_ROPE) ** -0.5 * math.log2(math.e)

COL_Q = 0
COL_KV = COL_Q + Q_RANK
COL_U = COL_KV + KV_RANK
COL_V = COL_U + GMLP_WIDTH
COL_KR = COL_V + GMLP_WIDTH
IN_COLS_PAD = COL_KR + LANES
PK_R1 = QK_NOPE
PK_R2 = QK_NOPE + ROPE_HALF
PK_END = QK_NOPE + QK_ROPE

BF16 = jnp.bfloat16
F32 = jnp.float32


def _rms(x, g):
    return x * lax.rsqrt(jnp.mean(x * x, axis=-1, keepdims=True) + EPS) * g


def _dot(a, b):
    return jnp.dot(a, b, preferred_element_type=F32)


def _dot_nt(a, b):
    return lax.dot_general(a, b, (((1,), (1,)), ((), ())), preferred_element_type=F32)


def _rope_packed(t, c_mul, s_up, s_down):
    return (t * c_mul
            + pltpu.roll(t, LANES - ROPE_HALF, axis=1) * s_up
            + pltpu.roll(t, ROPE_HALF, axis=1) * s_down)


def _front_kernel(absorbed, *refs):
    if absorbed:
        (x_ref, cos_ref, sin_ref, g1_ref, w_in_ref, gq_ref, wq_ref, gkv_ref, wuk_ref, vg_ref, vb_ref,
         qlat_ref, qrope_ref, ckv_ref, kr_ref, u_ref, v_ref) = refs
    else:
        (x_ref, cos_ref, sin_ref, cost_ref, sint_ref, g1_ref, w_in_ref, gq_ref, wq_ref, gkv_ref, wuk_ref,
         wuvt_ref, vg_ref, vb_ref, qt_ref, kpk_ref, ckv_ref, vt_ref, kr_ref, u_ref, v_ref) = refs
    x = x_ref[...]
    h = _rms(x, g1_ref[...]).astype(BF16)
    z = _dot(h, w_in_ref[...])

    cos_t = cos_ref[...]
    sin_t = sin_ref[...]
    lane = lax.broadcasted_iota(jnp.int32, cos_t.shape, 1)
    in_r1 = (lane >= PK_R1) & (lane < PK_R2)
    in_r2 = (lane >= PK_R2) & (lane < PK_END)

    ckv = _rms(z[:, COL_KV:COL_U], gkv_ref[...])
    ckv_ref[...] = ckv
    k_c = jnp.where(in_r1 | in_r2, cos_t, 0.0)
    k_up = jnp.where(in_r1, -sin_t, 0.0)
    k_dn = jnp.where(in_r2, sin_t, 0.0)
    kr = _rope_packed(z[:, COL_KR:COL_KR + LANES], k_c, k_up, k_dn)
    if absorbed:
        kr_ref[...] = kr[:, PK_R1:PK_END]
    else:
        kr_ref[...] = jnp.transpose(kr)[PK_R1:PK_END]

    cqn = _rms(z[:, COL_Q:COL_KV], gq_ref[...]).astype(BF16)
    if absorbed:
        qp = _dot(cqn, wq_ref[...])
        q_c = jnp.where(lane < PK_R1, Q_SCALE, jnp.where(lane < PK_END, Q_SCALE * cos_t, 0.0))
        q_up = jnp.where(in_r1, -Q_SCALE * sin_t, 0.0)
        q_dn = jnp.where(in_r2, Q_SCALE * sin_t, 0.0)
        for hd in range(N_HEADS):
            qh = _rope_packed(qp[:, hd * LANES:(hd + 1) * LANES], q_c, q_up, q_dn)
            qlat_ref[hd] = _dot(qh.astype(BF16), wuk_ref[hd])
            qrope_ref[hd] = qh[:, PK_R1:PK_END]
    else:
        qpt = _dot_nt(wq_ref[...], cqn)
        c_t = cost_ref[...] * Q_SCALE
        s_t = sint_ref[...] * Q_SCALE
        zpad = jnp.zeros((LANES - PK_END, qpt.shape[1]), F32)
        for hd in range(N_HEADS):
            blk = qpt[hd * LANES:(hd + 1) * LANES]
            x1 = blk[PK_R1:PK_R2]
            x2 = blk[PK_R2:PK_END]
            qt = jnp.concatenate([blk[:PK_R1] * Q_SCALE, x1 * c_t - x2 * s_t, x1 * s_t + x2 * c_t, zpad],
                                 axis=0)
            qt_ref[hd] = qt.astype(BF16)
        ckv_b = ckv.astype(BF16)
        knope = _dot(ckv_b, wuk_ref[...])
        for hd in range(N_HEADS):
            kpk_ref[hd] = (knope[:, hd * LANES:(hd + 1) * LANES] + kr).astype(BF16)
        vt_ref[...] = _dot_nt(wuvt_ref[...], ckv_b).astype(BF16)

    zg = z[:, COL_U:COL_KR]
    zg = 0.5 * zg * (1.0 + lax.erf(zg * math.sqrt(0.5)))
    u_ref[...] = zg[:, :GMLP_WIDTH]
    vv = zg[:, GMLP_WIDTH:]
    mu = jnp.mean(vv, axis=-1, keepdims=True)
    vc = vv - mu
    v_ref[...] = vc * lax.rsqrt(jnp.mean(vc * vc, axis=-1, keepdims=True) + EPS) * vg_ref[...] + vb_ref[...]


def _const_spec(shape):
    nd = len(shape)
    return pl.BlockSpec(shape, lambda i, _nd=nd: (0,) * _nd)


def _front(x, tables, wts, *, absorbed, tm, seq=None):
    n_tok, d_model = x.shape
    grid = (n_tok // tm,)
    row = lambda w: pl.BlockSpec((tm, w), lambda i: (i, 0))
    head = lambda w, dt: (jax.ShapeDtypeStruct((N_HEADS, n_tok, w), dt),
                          pl.BlockSpec((N_HEADS, tm, w), lambda i: (0, i, 0)))
    flat = lambda w, dt: (jax.ShapeDtypeStruct((n_tok, w), dt), row(w))
    tab_blocks = tables[0].shape[0] // tm
    table_specs = [pl.BlockSpec((tm, LANES), lambda i: (i % tab_blocks, 0))] * 2
    if absorbed:
        outs = [head(KV_RANK, F32), head(QK_ROPE, F32), flat(KV_RANK, F32), flat(QK_ROPE, F32),
                flat(GMLP_WIDTH, F32), flat(GMLP_WIDTH, F32)]
    else:
        spb = seq // tm
        nb = n_tok // seq
        table_specs += [pl.BlockSpec((ROPE_HALF, tm), lambda i: (0, i % spb))] * 2
        qt = (jax.ShapeDtypeStruct((nb, N_HEADS, LANES, seq), BF16),
              pl.BlockSpec((None, N_HEADS, LANES, tm), lambda i: (i // spb, 0, 0, i % spb)))
        vt = (jax.ShapeDtypeStruct((nb, MLA_WIDTH, seq), BF16),
              pl.BlockSpec((None, MLA_WIDTH, tm), lambda i: (i // spb, 0, i % spb)))
        krt = (jax.ShapeDtypeStruct((nb, QK_ROPE, seq), F32),
               pl.BlockSpec((None, QK_ROPE, tm), lambda i: (i // spb, 0, i % spb)))
        outs = [qt, head(LANES, BF16), flat(KV_RANK, F32), vt, krt,
                flat(GMLP_WIDTH, F32), flat(GMLP_WIDTH, F32)]
    ins = [x] + list(tables) + list(wts)
    in_specs = [row(d_model)] + table_specs + [_const_spec(w.shape) for w in wts]
    return pl.pallas_call(
        functools.partial(_front_kernel, absorbed),
        grid=grid,
        in_specs=in_specs,
        out_specs=[o[1] for o in outs],
        out_shape=[o[0] for o in outs],
        compiler_params=pltpu.CompilerParams(dimension_semantics=("parallel",),
                                             vmem_limit_bytes=VMEM_LIMIT),
        name="front_absorbed" if absorbed else "front_prompt",
    )(*ins)


ATTN_UNIT_COLS = 2 * MXU_DIM
ATTN_UNIT_KEYS = 256
SOFTMAX_ROW_CHUNKS = 8
def _prompt_attn_kernel(qi_ref, ki_ref, kind_ref, k_ref, qt_ref, vt_ref, a_ref, m_ref, l_ref, acc_ref):
    p_id = pl.program_id(1)
    ki = ki_ref[p_id]
    kind = kind_ref[p_id]
    blk_k = k_ref.shape[1]
    blk_q = qt_ref.shape[2]
    qw = ATTN_UNIT_COLS

    @pl.when(ki == 0)
    def _():
        m_ref[...] = jnp.full(m_ref.shape, -jnp.inf, F32)
        l_ref[...] = jnp.zeros(l_ref.shape, F32)
        acc_ref[...] = jnp.zeros(acc_ref.shape, F32)

    def step(delta):
        kw = min(ATTN_UNIT_KEYS, blk_k)
        first_q = lambda c: delta + c * qw
        units = [(k0, c, hd) for k0 in range(0, blk_k, kw) for c in range(blk_q // qw)
                 if delta is None or k0 < first_q(c) + qw for hd in range(N_HEADS)]

        def scores(k0, c, hd):
            st = _dot(k_ref[hd, k0:k0 + kw, :], qt_ref[hd, :, c * qw:(c + 1) * qw])
            if delta is not None and k0 + kw - 1 > first_q(c):
                kpos = lax.broadcasted_iota(jnp.int32, (kw, qw), 0) + k0
                qpos = lax.broadcasted_iota(jnp.int32, (kw, qw), 1) + first_q(c)
                st = jnp.where(kpos <= qpos, st, -jnp.inf)
            return st

        def rows(st, r):
            rk = st.shape[0] // SOFTMAX_ROW_CHUNKS
            return st[r * rk:(r + 1) * rk]

        def fold8(x, op):
            return op(x.reshape(x.shape[0] // SUBLANES, SUBLANES, x.shape[1]), axis=0)

        n_units = len(units)
        st_cur = scores(*units[0])
        mloc_cur = jnp.max(st_cur, axis=0, keepdims=True)
        st_nxt = scores(*units[1])
        for idx, (k0, c, hd) in enumerate(units):
            cols = slice(c * qw, (c + 1) * qw)
            m_prev = m_ref[hd, :, cols]
            m_new = jnp.maximum(m_prev, mloc_cur)
            alpha = jnp.exp2(m_prev - m_new)
            st_nn = scores(*units[idx + 2]) if idx + 2 < n_units else None
            mx8 = None
            pts = []
            for r in range(SOFTMAX_ROW_CHUNKS):
                if idx + 1 < n_units:
                    part = fold8(rows(st_nxt, r), jnp.max)
                    mx8 = part if mx8 is None else jnp.maximum(mx8, part)
                pts.append(jnp.exp2(rows(st_cur, r) - m_new).astype(BF16))
            pt = jnp.concatenate(pts, axis=0)
            mloc_nxt = None if mx8 is None else jnp.max(mx8, axis=0, keepdims=True)
            vt = jnp.concatenate([vt_ref[hd * V_HEAD:(hd + 1) * V_HEAD, k0:k0 + kw],
                                  jnp.ones((BF16_ROWS, kw), BF16)], axis=0)
            pv = _dot(vt, pt)
            l_ref[hd, :, cols] = alpha * l_ref[hd, :, cols] + pv[V_HEAD:V_HEAD + 1]
            acc_ref[hd, :, cols] = alpha * acc_ref[hd, :, cols] + pv[:V_HEAD]
            m_ref[hd, :, cols] = m_new
            st_cur, mloc_cur, st_nxt = st_nxt, mloc_nxt, st_nn

    @pl.when(kind == 0)
    def _():
        step(None)

    for d in range(blk_k // blk_q):
        @pl.when(kind == 1 + d)
        def _():
            step(d * blk_q)
            at = jnp.concatenate([acc_ref[hd] * (1.0 / l_ref[hd]) for hd in range(N_HEADS)], axis=0)
            a_ref[...] = jnp.transpose(at)


def _prompt_attend(k_pk, q_t, v_t, *, batch, seq, blk_q, blk_k):
    assert blk_k % blk_q == 0 and seq % blk_k == 0
    nq, nk = seq // blk_q, seq // blk_k
    pairs = []
    for q in range(nq):
        k_last = (q * blk_q) // blk_k
        pairs += [(q, k, 0) for k in range(k_last)]
        pairs.append((q, k_last, 1 + (q * blk_q - k_last * blk_k) // blk_q))
    qi, ki, kind = (jnp.asarray([p[i] for p in pairs], jnp.int32) for i in range(3))
    n_tok = batch * seq
    grid_spec = pltpu.PrefetchScalarGridSpec(
        num_scalar_prefetch=3,
        grid=(batch, len(pairs)),
        in_specs=[
            pl.BlockSpec((N_HEADS, blk_k, LANES), lambda b, p, qi, ki, kd: (0, b * nk + ki[p], 0)),
            pl.BlockSpec((None, N_HEADS, LANES, blk_q), lambda b, p, qi, ki, kd: (b, 0, 0, qi[p])),
            pl.BlockSpec((None, MLA_WIDTH, blk_k), lambda b, p, qi, ki, kd: (b, 0, ki[p])),
        ],
        out_specs=pl.BlockSpec((blk_q, MLA_WIDTH), lambda b, p, qi, ki, kd: (b * nq + qi[p], 0)),
        scratch_shapes=[pltpu.VMEM((N_HEADS, 1, blk_q), F32), pltpu.VMEM((N_HEADS, 1, blk_q), F32),
                        pltpu.VMEM((N_HEADS, V_HEAD, blk_q), F32)],
    )
    return pl.pallas_call(
        _prompt_attn_kernel,
        grid_spec=grid_spec,
        out_shape=jax.ShapeDtypeStruct((n_tok, MLA_WIDTH), F32),
        compiler_params=pltpu.CompilerParams(dimension_semantics=("parallel", "arbitrary"),
                                             vmem_limit_bytes=VMEM_LIMIT),
        name="prompt_attend",
    )(qi, ki, kind, k_pk, q_t, v_t)


N_SLOTS = 3
DECODE_SPLIT = 4


def _local_softmax(s, v_b):
    m = jnp.max(s, axis=-1, keepdims=True)
    p = jnp.exp2(s - m)
    return m, jnp.sum(p, axis=-1, keepdims=True), _dot(p.astype(BF16), v_b)


def _merge_softmax(parts, m_ref, l_ref, acc_ref):
    m_run = m_ref[...]
    m_new = m_run
    for m, _, _ in parts:
        m_new = jnp.maximum(m_new, m)
    w_run = jnp.exp2(m_run - m_new)
    l_new = w_run * l_ref[...]
    acc_new = w_run * acc_ref[...]
    for m, l, acc in parts:
        w = jnp.exp2(m - m_new)
        l_new = l_new + w * l
        acc_new = acc_new + w * acc
    m_ref[...] = m_new
    l_ref[...] = l_new
    acc_ref[...] = acc_new


def _decode_attn_kernel(n_pg, single_chunk, pt_ref, ql_ref, qr_ref, cn_ref, rn_ref, ckv_hbm, krt_hbm, o_ref,
                        kbuf, rbuf, sems, m_ref, l_ref, acc_ref):
    b = pl.program_id(0)
    j = pl.program_id(1)
    nj = pl.num_programs(1)
    chunk = b * nj + j
    n_chunks = pl.num_programs(0) * nj
    slot = chunk % N_SLOTS

    def page_copies(ck, sl, i):
        page = pt_ref[ck * n_pg + i]
        rows = pl.ds(i * PAGE_SIZE, PAGE_SIZE)
        return (pltpu.make_async_copy(ckv_hbm.at[0, page], kbuf.at[sl, rows], sems.at[0, sl]),
                pltpu.make_async_copy(krt_hbm.at[0, page], rbuf.at[sl, :, rows], sems.at[1, sl]))

    def start_pages(ck, sl, pages):
        for i in pages:
            for cp in page_copies(ck, sl, i):
                cp.start()

    def wait_chunk(sl):
        for i in range(n_pg):
            for cp in page_copies(0, sl, i):
                cp.wait()

    ahead = N_SLOTS - 1

    @pl.when(chunk == 0)
    def _():
        for ck in range(ahead):
            start_pages(ck, ck, range(n_pg))

    nxt = jnp.minimum(chunk + ahead, n_chunks - 1)
    nxt_slot = (chunk + ahead) % N_SLOTS

    if not single_chunk:
        @pl.when(j == 0)
        def _():
            m_ref[...] = jnp.full(m_ref.shape, -jnp.inf, F32)
            l_ref[...] = jnp.zeros(l_ref.shape, F32)
            acc_ref[...] = jnp.zeros(acc_ref.shape, F32)

    n_rows = N_HEADS * ql_ref.shape[1]
    ql = ql_ref[...].reshape(n_rows, KV_RANK).astype(BF16)
    qr = qr_ref[...].reshape(n_rows, QK_ROPE).astype(BF16)

    wait_chunk(slot)

    keys = n_pg * PAGE_SIZE // DECODE_SPLIT
    pages_per_part = n_pg // DECODE_SPLIT
    kbs, scores = [], []
    for c in range(DECODE_SPLIT):
        kb = kbuf[slot, c * keys:(c + 1) * keys, :].astype(BF16)
        rbt = rbuf[slot, :, c * keys:(c + 1) * keys].astype(BF16)
        kbs.append(kb)
        scores.append(_dot_nt(ql, kb) + _dot(qr, rbt))
        start_pages(nxt, nxt_slot, range(c * pages_per_part, (c + 1) * pages_per_part))
    def new_rows_part():
        t_new = cn_ref.shape[0]
        cb = cn_ref[...].astype(BF16)
        s_new = _dot_nt(ql, cb) + _dot_nt(qr, rn_ref[...].astype(BF16))
        r_pos = lax.broadcasted_iota(jnp.int32, s_new.shape, 0) % t_new
        c_pos = lax.broadcasted_iota(jnp.int32, s_new.shape, 1)
        return _local_softmax(jnp.where(c_pos <= r_pos, s_new, -jnp.inf), cb)

    parts = [_local_softmax(s, kb) for s, kb in zip(scores, kbs)]
    if single_chunk:
        parts.append(new_rows_part())
        m_all = functools.reduce(jnp.maximum, [m for m, _, _ in parts])
        weights = [jnp.exp2(m - m_all) for m, _, _ in parts]
        l_all = sum(w * l for w, (_, l, _) in zip(weights, parts))
        acc_all = sum(w * acc for w, (_, _, acc) in zip(weights, parts))
        o_ref[...] = (acc_all * (1.0 / l_all)).reshape(o_ref.shape)
    else:
        _merge_softmax(parts, m_ref, l_ref, acc_ref)

        @pl.when(j == nj - 1)
        def _():
            _merge_softmax([new_rows_part()], m_ref, l_ref, acc_ref)
            o = acc_ref[...] * (1.0 / l_ref[...])
            o_ref[...] = o.reshape(o_ref.shape)

    @pl.when(chunk == n_chunks - 1)
    def _():
        for extra in range(1, ahead + 1):
            wait_chunk((chunk + extra) % N_SLOTS)


def _decode_attend(q_lat, q_rope, ckv_new, kr_new, cache_ckv, cache_krope_t, page_table, *, t_new, n_pg):
    dec_batch, n_pages = page_table.shape
    n_tok = dec_batch * t_new
    assert n_pages % n_pg == 0 and dec_batch * (n_pages // n_pg) >= N_SLOTS
    pt_flat = page_table.reshape(-1)
    in_specs = [
        pl.BlockSpec((N_HEADS, t_new, KV_RANK), lambda b, j, pt: (0, b, 0)),
        pl.BlockSpec((N_HEADS, t_new, QK_ROPE), lambda b, j, pt: (0, b, 0)),
        pl.BlockSpec((t_new, KV_RANK), lambda b, j, pt: (b, 0)),
        pl.BlockSpec((t_new, QK_ROPE), lambda b, j, pt: (b, 0)),
        pl.BlockSpec(memory_space=pl.ANY),
        pl.BlockSpec(memory_space=pl.ANY),
    ]
    rows = N_HEADS * t_new
    grid_spec = pltpu.PrefetchScalarGridSpec(
        num_scalar_prefetch=1,
        grid=(dec_batch, n_pages // n_pg),
        in_specs=in_specs,
        out_specs=pl.BlockSpec((N_HEADS, t_new, KV_RANK), lambda b, j, pt: (0, b, 0)),
        scratch_shapes=[pltpu.VMEM((N_SLOTS, n_pg * PAGE_SIZE, KV_RANK), F32),
                        pltpu.VMEM((N_SLOTS, QK_ROPE, n_pg * PAGE_SIZE), F32),
                        pltpu.SemaphoreType.DMA((2, N_SLOTS)),
                        pltpu.VMEM((rows, 1), F32), pltpu.VMEM((rows, 1), F32),
                        pltpu.VMEM((rows, KV_RANK), F32)],
    )
    return pl.pallas_call(
        functools.partial(_decode_attn_kernel, n_pg, n_pages == n_pg),
        grid_spec=grid_spec,
        out_shape=jax.ShapeDtypeStruct((N_HEADS, n_tok, KV_RANK), F32),
        compiler_params=pltpu.CompilerParams(dimension_semantics=("arbitrary", "arbitrary"),
                                             vmem_limit_bytes=VMEM_LIMIT),
        name="decode_attend",
    )(pt_flat, q_lat, q_rope, ckv_new, kr_new, cache_ckv, cache_krope_t)


def _back_kernel(latent_attn, *refs):
    if latent_attn:
        (o_ref, u_ref, v_ref, x_ref, wuv_ref, wmix_ref, bmix_ref, ga_ref, gg_ref, wout_ref,
         g2_ref, wgate_ref, wup_ref, wdown_ref, gf_ref, y_ref) = refs
        o_flat = jnp.concatenate([o_ref[hd].astype(BF16) for hd in range(N_HEADS)], axis=1)
        a = _dot(o_flat, wuv_ref[...])
    else:
        (a_ref, u_ref, v_ref, x_ref, wmix_ref, bmix_ref, ga_ref, gg_ref, wout_ref,
         g2_ref, wgate_ref, wup_ref, wdown_ref, gf_ref, y_ref) = refs
        a = a_ref[...]
    tm = x_ref.shape[0]
    an = _rms(a, ga_ref[...])

    lane = lax.broadcasted_iota(jnp.int32, (CHUNK, GMLP_WIDTH), 1)
    wmix = wmix_ref[...]
    mixes = []
    for c in range(tm // CHUNK):
        vck = v_ref[c * CHUNK:(c + 1) * CHUNK, :].astype(BF16)
        bd = jnp.concatenate(
            [jnp.where((lane >= g * GROUP_DIM) & (lane < (g + 1) * GROUP_DIM), vck, jnp.zeros_like(vck))
             for g in range(N_GROUPS)], axis=0)
        mixes.append(_dot(wmix, bd) + bmix_ref[...])
    mix = mixes[0] if len(mixes) == 1 else jnp.concatenate(mixes, axis=0)
    gn = _rms(u_ref[...] * mix, gg_ref[...])

    merged = jnp.concatenate([an, gn], axis=1).astype(BF16)
    x1 = x_ref[...] + _dot(merged, wout_ref[...])
    h2 = _rms(x1, g2_ref[...]).astype(BF16)
    gate = _dot(h2, wgate_ref[...])
    up = _dot(h2, wup_ref[...])
    act = (gate * (1.0 / (1.0 + jnp.exp(-gate))) * up).astype(BF16)
    x2 = x1 + _dot(act, wdown_ref[...])
    y_ref[...] = _rms(x2, gf_ref[...])


def _back(attn, u, v, x, wts, *, tm, name):
    n_tok, d_model = x.shape
    latent_attn = attn.ndim == 3
    row = lambda w: pl.BlockSpec((tm, w), lambda i: (i, 0))
    attn_spec = pl.BlockSpec((N_HEADS, tm, KV_RANK), lambda i: (0, i, 0)) if latent_attn else row(MLA_WIDTH)
    in_specs = [attn_spec, row(GMLP_WIDTH), row(GMLP_WIDTH), row(d_model)]
    in_specs += [pl.BlockSpec(w.shape, lambda i, _nd=w.ndim: (0,) * _nd, pipeline_mode=pl.Buffered(1))
                 for w in wts]
    return pl.pallas_call(
        functools.partial(_back_kernel, latent_attn),
        grid=(n_tok // tm,),
        in_specs=in_specs,
        out_specs=row(d_model),
        out_shape=jax.ShapeDtypeStruct((n_tok, d_model), F32),
        compiler_params=pltpu.CompilerParams(dimension_semantics=("parallel",),
                                             vmem_limit_bytes=VMEM_LIMIT),
        name=name,
    )(attn, u, v, x, *wts)


def _rope_angles(pos):
    inv = ROPE_BASE ** (-jnp.arange(ROPE_HALF, dtype=F32) / ROPE_HALF)
    ang = pos.astype(F32)[:, None] * inv[None, :]
    return jnp.cos(ang), jnp.sin(ang)


def _lane_tiled(t, reps_rows):
    return jnp.tile(t, (reps_rows, LANES // ROPE_HALF))


def kernel(x_prompt, x_sample, cache_ckv, cache_krope, page_table, norm1_g, w_in, q_norm_g, w_q_up, kv_norm_g, w_uk, w_uv, v_norm_g, v_norm_b, w_spatial, b_spatial, out_norm_mla_g, out_norm_gmlp_g, w_out, norm2_g, w_gate, w_up, w_down, final_norm_g):
    batch, seq, d_model = x_prompt.shape
    dec_batch, t_new, _ = x_sample.shape
    depth = w_in.shape[0]
    past_len = page_table.shape[1] * PAGE_SIZE
    assert depth == 1 and seq % CHUNK == 0 and t_new <= CHUNK and CHUNK % t_new == 0
    assert w_in.shape[1:] == (d_model, Q_RANK + KV_RANK + QK_ROPE + 2 * GMLP_WIDTH)
    assert w_q_up.shape[1:] == (Q_RANK, N_HEADS * (QK_NOPE + QK_ROPE))
    assert w_uk.shape[1:] == (KV_RANK, N_HEADS, QK_NOPE) and w_uv.shape[1:] == (KV_RANK, N_HEADS, V_HEAD)
    assert w_spatial.shape[1:] == (N_GROUPS, CHUNK, CHUNK) and b_spatial.shape[1:] == (N_GROUPS, CHUNK)
    assert cache_ckv.shape[2:] == (PAGE_SIZE, KV_RANK) and cache_krope.shape[2:] == (PAGE_SIZE, QK_ROPE)
    assert page_table.shape[0] == dec_batch and w_out.shape[1:] == (MLA_WIDTH + GMLP_WIDTH, d_model)

    xp = x_prompt.reshape(batch * seq, d_model)
    xs = x_sample.reshape(dec_batch * t_new, d_model)
    cos_p, sin_p = _rope_angles(jnp.arange(seq))
    cos_s, sin_s = _rope_angles(past_len + jnp.arange(t_new))
    tm_front, tm_tok, n_pg = 1024, 512, page_table.shape[1]
    attn_blk_q, attn_blk_k = 1024, 1024
    assert tm_tok % t_new == 0 and seq % tm_front == 0
    assert (batch * seq) % tm_tok == 0 and (dec_batch * t_new) % tm_tok == 0
    tables_p = (_lane_tiled(cos_p, 1), _lane_tiled(sin_p, 1), cos_p.T, sin_p.T)
    tables_s = (_lane_tiled(cos_s, tm_tok // t_new), _lane_tiled(sin_s, tm_tok // t_new))

    l = 0
    row2 = lambda g: g.reshape(1, -1)
    wi = w_in[l]
    off_kr = Q_RANK + KV_RANK
    off_g = off_kr + QK_ROPE
    w_in_p = jnp.concatenate(
        [wi[:, :off_kr], wi[:, off_g:], jnp.zeros((d_model, PK_R1), wi.dtype), wi[:, off_kr:off_g],
         jnp.zeros((d_model, LANES - PK_END), wi.dtype)], axis=1).astype(BF16)
    wq = w_q_up[l].reshape(Q_RANK, N_HEADS, QK_NOPE + QK_ROPE)
    wq_p = jnp.concatenate([wq, jnp.zeros((Q_RANK, N_HEADS, LANES - PK_END), wq.dtype)], axis=2)
    wq_p = wq_p.reshape(Q_RANK, N_HEADS * LANES).astype(BF16)
    wuk = jnp.transpose(w_uk[l], (1, 2, 0))
    wuk_h = jnp.concatenate([wuk, jnp.zeros((N_HEADS, LANES - QK_NOPE, KV_RANK), wuk.dtype)],
                            axis=1).astype(BF16)
    wuk_cols = jnp.transpose(wuk_h, (2, 0, 1)).reshape(KV_RANK, N_HEADS * LANES)
    front_tail = (row2(v_norm_g[l]), row2(v_norm_b[l]))

    eye_h = jnp.eye(N_HEADS, dtype=w_uv.dtype)
    wuv_bd = jnp.einsum('rhv,hg->hrgv', w_uv[l], eye_h).reshape(N_HEADS * KV_RANK, MLA_WIDTH).astype(BF16)
    tril = jnp.tril(jnp.ones((CHUNK, CHUNK), dtype=bool))
    wmix_p = jnp.where(tril, w_spatial[l], 0)
    bmix_p = jnp.repeat(b_spatial[l].T, GROUP_DIM, axis=1)
    reps = CHUNK // t_new
    tril_s = jnp.tril(jnp.ones((t_new, t_new), dtype=bool))
    w_small = jnp.where(tril_s, w_spatial[l][:, :t_new, :t_new], 0)
    pos = jnp.arange(CHUNK) // t_new
    same_elem = pos[:, None] == pos[None, :]
    wmix_s = jnp.where(same_elem, jnp.tile(w_small, (1, reps, reps)), 0)
    bmix_s = jnp.tile(jnp.repeat(b_spatial[l][:, :t_new].T, GROUP_DIM, axis=1), (reps, 1))
    cat_groups = lambda w: jnp.transpose(w, (1, 0, 2)).reshape(CHUNK, N_GROUPS * CHUNK).astype(BF16)
    back_tail = (row2(out_norm_mla_g[l]), row2(out_norm_gmlp_g[l]), w_out[l].astype(BF16), row2(norm2_g[l]),
                 w_gate[l].astype(BF16), w_up[l].astype(BF16), w_down[l].astype(BF16), row2(final_norm_g))

    wuv_t = w_uv[l].reshape(KV_RANK, MLA_WIDTH).T.astype(BF16)
    wts_p = (row2(norm1_g[l]), w_in_p, row2(q_norm_g[l]), wq_p.T, row2(kv_norm_g[l]), wuk_cols, wuv_t) + front_tail
    q_t, k_pk, ckv_p, v_t, kr_p, u_p, v_p = _front(xp, tables_p, wts_p, absorbed=False, tm=tm_front, seq=seq)
    a_p = _prompt_attend(k_pk, q_t, v_t, batch=batch, seq=seq, blk_q=attn_blk_q, blk_k=attn_blk_k)
    y_p = _back(a_p, u_p, v_p, xp, (cat_groups(wmix_p), bmix_p) + back_tail, tm=tm_tok, name="back_prompt")

    wts_s = (row2(norm1_g[l]), w_in_p, row2(q_norm_g[l]), wq_p, row2(kv_norm_g[l]), wuk_h) + front_tail
    q_lat, q_rope, ckv_s, kr_s, u_s, v_s = _front(xs, tables_s, wts_s, absorbed=True, tm=tm_tok)
    o_s = _decode_attend(q_lat, q_rope, ckv_s, kr_s, cache_ckv, jnp.swapaxes(cache_krope, 2, 3), page_table,
                         t_new=t_new, n_pg=n_pg)
    y_s = _back(o_s, u_s, v_s, xs, (wuv_bd, cat_groups(wmix_s), bmix_s) + back_tail, tm=tm_tok, name="back_decode")

    return (y_p.reshape(batch, seq, d_model),
            y_s.reshape(dec_batch, t_new, d_model),
            ckv_p.reshape(depth, batch, seq, KV_RANK),
            jnp.swapaxes(kr_p, 1, 2).reshape(depth, batch, seq, QK_ROPE),
            ckv_s.reshape(depth, dec_batch, t_new, KV_RANK),
            kr_s.reshape(depth, dec_batch, t_new, QK_ROPE),
            v_s.reshape(depth, dec_batch, t_new, GMLP_WIDTH))
```
